```python
import math
import jax
import jax.numpy as jnp
from jax import lax
import numpy as np

D_MODEL = 1024
BATCH = 2
SEQ = 8192
DEPTH = 4

GRID_W = 64
CTX_LEN = 256
GROUP_W = 256
D_MIX = 4 * GROUP_W
D_FF = 4 * D_MODEL
N_MOD = 6
Q_BLOCK = 128
ROPE_THETA = 10000.0
NORM_EPS = 1e-6
CONV_W = 4

SSD_HEADS = 4
SSD_HEAD_DIM = GROUP_W // SSD_HEADS
SSD_GROUPS = 2
SSD_STATE = 128
SSD_CHUNK = 128
SSD_XBC = GROUP_W + 2 * SSD_GROUPS * SSD_STATE
SSD_COLS = GROUP_W + SSD_XBC + 2 * SSD_HEADS

GQA_HEADS = 4
GQA_KV_HEADS = 2
GQA_HEAD_DIM = GROUP_W // GQA_HEADS
GQA_COLS = GROUP_W + 2 * GQA_KV_HEADS * GQA_HEAD_DIM

LRU_WIDTH = GROUP_W
LRU_BLOCKS = 4
LRU_BLOCK_W = LRU_WIDTH // LRU_BLOCKS
LRU_C = 8.0
LRU_COLS = 2 * LRU_WIDTH

DIFF_HEADS = 4
DIFF_V_DIM = GROUP_W // DIFF_HEADS
DIFF_QK_DIM = DIFF_V_DIM // 2
DIFF_COLS = 3 * GROUP_W

IN_COLS = SSD_COLS + GQA_COLS + LRU_COLS + DIFF_COLS
IN_SPLITS = (SSD_COLS, SSD_COLS + GQA_COLS, SSD_COLS + GQA_COLS + LRU_COLS)

kernel_name = 'hybrid_parallel_heads_diffusion_trunk'


def rms_norm(x, g):
    xf = x.astype(jnp.float32)
    y = xf * lax.rsqrt(jnp.mean(xf * xf, axis=-1, keepdims=True) + NORM_EPS)
    return (y * g.astype(jnp.float32)).astype(x.dtype)


def modulate(x, shift, scale):
    return x * (1 + scale) + shift


def dwconv(x, w, b):
    k = w.shape[0]
    pad_l = k // 2
    y = lax.conv_general_dilated(x, w[:, None, :], window_strides=(1,), padding=[(pad_l, k - 1 - pad_l)],
                                 dimension_numbers=('NWC', 'WIO', 'NWC'), feature_group_count=x.shape[-1])
    return y + b


def _rotate(x, pos):
    half = x.shape[-1] // 2
    freqs = ROPE_THETA ** (-jnp.arange(half, dtype=jnp.float32) / half)
    ang = pos.astype(jnp.float32)[:, None] * freqs[None, :]
    cos = jnp.cos(ang)[None, :, None, :]
    sin = jnp.sin(ang)[None, :, None, :]
    x1, x2 = x[..., :half], x[..., half:]
    return jnp.concatenate([x1 * cos - x2 * sin, x1 * sin + x2 * cos], axis=-1)


def axial_rope(x, row, col):
    m = x.shape[-1] // 2
    xf = x.astype(jnp.float32)
    return jnp.concatenate([_rotate(xf[..., :m], row), _rotate(xf[..., m:], col)], axis=-1).astype(x.dtype)


def block_attention(q, k, v):
    b, lq, hq, d = q.shape
    hkv, dv = k.shape[2], v.shape[-1]
    rep = hq // hkv
    nb = lq // Q_BLOCK
    qb = jnp.moveaxis(q.reshape(b, nb, Q_BLOCK, hkv, rep, d), 1, 0)
    scale = d ** -0.5

    def one_block(qblk):
        s = jnp.einsum('bqgrd,bkgd->bgrqk', qblk, k, preferred_element_type=jnp.float32) * scale
        pr = jax.nn.softmax(s, axis=-1).astype(v.dtype)
        return jnp.einsum('bgrqk,bkgv->bqgrv', pr, v)

    o = lax.map(one_block, qb)
    return jnp.moveaxis(o, 0, 1).reshape(b, lq, hq, dv)


def ssd_chunked_scan(x, dt, a, bm, cm, h0):
    b, L, H, P = x.shape
    nc = L // SSD_CHUNK

    def chunk(t):
        return t.reshape((b, nc, SSD_CHUNK) + t.shape[2:])

    x, dt, bm, cm = chunk(x), chunk(dt), chunk(bm), chunk(cm)
    a_cum = jnp.cumsum(dt * a, axis=2)
    xdt = x * dt[..., None]
    seg = a_cum[:, :, :, None, :] - a_cum[:, :, None, :, :]
    lower = jnp.tril(jnp.ones((SSD_CHUNK, SSD_CHUNK), dtype=bool))[None, None, :, :, None]
    decay = jnp.exp(jnp.where(lower, seg, -jnp.inf))
    scores = jnp.einsum('bclhn,bcshn->bclsh', cm, bm) * decay
    y_diag = jnp.einsum('bclsh,bcshp->bclhp', scores, xdt)
    to_end = jnp.exp(a_cum[:, :, -1:, :] - a_cum)
    states = jnp.einsum('bclhn,bclhp->bchpn', bm * to_end[..., None], xdt)
    chunk_decay = jnp.exp(a_cum[:, :, -1, :])

    def step(h, inp):
        s, dcy = inp
        return h * dcy[:, :, None, None] + s, h

    h_last, h_in = lax.scan(step, h0, (jnp.moveaxis(states, 1, 0), jnp.moveaxis(chunk_decay, 1, 0)))
    h_in = jnp.moveaxis(h_in, 0, 1)
    y_off = jnp.einsum('bclhn,bchpn->bclhp', cm * jnp.exp(a_cum)[..., None], h_in)
    return (y_diag + y_off).reshape(b, L, H, P), h_last


def ssd_features(proj, conv_w, conv_b):
    b, L, _ = proj.shape
    z, xbc, dt = jnp.split(proj, [GROUP_W, GROUP_W + SSD_XBC], axis=-1)
    xbc = jax.nn.silu(dwconv(xbc, conv_w, conv_b)).astype(jnp.float32)
    xs, bm, cm = jnp.split(xbc, [GROUP_W, GROUP_W + SSD_GROUPS * SSD_STATE], axis=-1)
    rep = SSD_HEADS // SSD_GROUPS
    xs = xs.reshape(b, L, SSD_HEADS, SSD_HEAD_DIM)
    bm = jnp.repeat(bm.reshape(b, L, SSD_GROUPS, SSD_STATE), rep, axis=2)
    cm = jnp.repeat(cm.reshape(b, L, SSD_GROUPS, SSD_STATE), rep, axis=2)
    dt = dt.astype(jnp.float32).reshape(b, L, 2, SSD_HEADS)
    return z, xs, bm, cm, dt


def ssd_direction(xs, bm, cm, dt_raw, a_log, dt_bias, d_skip, h0, reverse):
    dt = jax.nn.softplus(dt_raw + dt_bias.astype(jnp.float32))
    a = -jnp.exp(a_log.astype(jnp.float32))
    if reverse:
        y, h = ssd_chunked_scan(jnp.flip(xs, 1), jnp.flip(dt, 1), a, jnp.flip(bm, 1), jnp.flip(cm, 1), h0)
        y = jnp.flip(y, 1)
    else:
        y, h = ssd_chunked_scan(xs, dt, a, bm, cm, h0)
    return y + d_skip.astype(jnp.float32)[:, None] * xs, h


def ssd_bidir(feats, a_log, dt_bias, d_skip, norm_g, h0_fwd, h0_bwd):
    z, xs, bm, cm, dt = feats
    b, L = xs.shape[:2]
    y_f, h_f = ssd_direction(xs, bm, cm, dt[:, :, 0], a_log[0], dt_bias[0], d_skip[0], h0_fwd, False)
    y_b, h_b = ssd_direction(xs, bm, cm, dt[:, :, 1], a_log[1], dt_bias[1], d_skip[1], h0_bwd, True)
    y = (y_f + y_b).reshape(b, L, GROUP_W) * jax.nn.silu(z.astype(jnp.float32))
    y = rms_norm(y.reshape(b, L, SSD_GROUPS, GROUP_W // SSD_GROUPS),
                 norm_g.reshape(SSD_GROUPS, GROUP_W // SSD_GROUPS)).reshape(b, L, GROUP_W)
    return y.astype(z.dtype), h_f, h_b


def lru_features(proj, conv_w, conv_b):
    gate, xr = jnp.split(proj, 2, axis=-1)
    return jax.nn.gelu(gate, approximate=True), dwconv(xr, conv_w, conv_b).astype(jnp.float32)


def block_diag(x, w, bias):
    b, L, _ = x.shape
    y = jnp.einsum('blnj,njk->blnk', x.reshape(b, L, LRU_BLOCKS, LRU_BLOCK_W), w.astype(jnp.float32))
    return y.reshape(b, L, LRU_WIDTH) + bias.astype(jnp.float32)


def linear_scan(a, u, h0):
    u = u.at[:, 0].add(a[:, 0] * h0)

    def combine(left, right):
        return left[0] * right[0], right[0] * left[1] + right[1]

    _, h = lax.associative_scan(combine, (a, u), axis=1)
    return h


def rglru_direction(xr, w_r, b_r, w_i, b_i, lam, h0, reverse):
    if reverse:
        xr = jnp.flip(xr, 1)
    r = jax.nn.sigmoid(block_diag(xr, w_r, b_r))
    i = jax.nn.sigmoid(block_diag(xr, w_i, b_i))
    log_a = -LRU_C * r * jax.nn.softplus(-lam.astype(jnp.float32))
    a = jnp.exp(log_a)
    u = jnp.sqrt(-jnp.expm1(2.0 * log_a)) * (i * xr)
    h = linear_scan(a, u, h0)
    h_last = h[:, -1]
    if reverse:
        h = jnp.flip(h, 1)
    return h, h_last


def gqa_qkv(proj, q_g, k_g):
    b, L, _ = proj.shape
    q, k, v = jnp.split(proj, [GROUP_W, GROUP_W + GQA_KV_HEADS * GQA_HEAD_DIM], axis=-1)
    q = rms_norm(q.reshape(b, L, GQA_HEADS, GQA_HEAD_DIM), q_g)
    k = rms_norm(k.reshape(b, L, GQA_KV_HEADS, GQA_HEAD_DIM), k_g)
    return q, k, v.reshape(b, L, GQA_KV_HEADS, GQA_HEAD_DIM)


def diff_qkv(proj, q_g, k_g):
    b, L, _ = proj.shape
    q, k, v = jnp.split(proj, 3, axis=-1)
    q = rms_norm(q.reshape(b, L, 2 * DIFF_HEADS, DIFF_QK_DIM), q_g)
    k = rms_norm(k.reshape(b, L, 2 * DIFF_HEADS, DIFF_QK_DIM), k_g)
    return q, k, v.reshape(b, L, DIFF_HEADS, DIFF_V_DIM)


def diff_attention(q, k, v, lam, lam_init, subln_g):
    o1 = block_attention(q[:, :, 0::2], k[:, :, 0::2], v)
    o2 = block_attention(q[:, :, 1::2], k[:, :, 1::2], v)
    o = (o1.astype(jnp.float32) - lam * o2.astype(jnp.float32))
    o = rms_norm(o, subln_g) * (1.0 - lam_init)
    b, L = o.shape[:2]
    return o.reshape(b, L, GROUP_W).astype(v.dtype)


def sq_relu_mlp(u, w1, w2):
    return jnp.square(jax.nn.relu(u @ w1)) @ w2


def trunk_layer(h_lat, h_ctx, c_act, cctx_act, p, row, col, lam_init, update_ctx):
    bsz, seq_len, _ = h_lat.shape
    dtype = h_lat.dtype
    mod_l = jnp.split((c_act @ p['w_mod'] + p['b_mod'])[:, None, :], N_MOD, axis=-1)
    mod_c = jnp.split(cctx_act @ p['w_mod'] + p['b_mod'], N_MOD, axis=-1)

    u_l = modulate(rms_norm(h_lat, p['norm1_g']), mod_l[0], mod_l[1])
    u_c = modulate(rms_norm(h_ctx, p['norm1_g']), mod_c[0], mod_c[1])
    pa_l, pb_l, pc_l, pd_l = jnp.split(u_l @ p['w_in'], IN_SPLITS, axis=-1)
    pa_c, pb_c, pc_c, pd_c = jnp.split(u_c @ p['w_in'], IN_SPLITS, axis=-1)

    feats_c = ssd_features(pa_c, p['ssd_conv_w'], p['ssd_conv_b'])
    feats_l = ssd_features(pa_l, p['ssd_conv_w'], p['ssd_conv_b'])
    h0 = jnp.zeros((bsz, SSD_HEADS, SSD_HEAD_DIM, SSD_STATE), jnp.float32)
    ya_c, hs_f, hs_b = ssd_bidir(feats_c, p['ssd_a_log'], p['ssd_dt_bias'], p['ssd_d'], p['ssd_norm_g'], h0, h0)
    ya_l, _, _ = ssd_bidir(feats_l, p['ssd_a_log'], p['ssd_dt_bias'], p['ssd_d'], p['ssd_norm_g'], hs_f, hs_b)

    qb_c, kb_c, vb_c = gqa_qkv(pb_c, p['gqa_q_norm_g'], p['gqa_k_norm_g'])
    qb_l, kb_l, vb_l = gqa_qkv(pb_l, p['gqa_q_norm_g'], p['gqa_k_norm_g'])
    qb_l, kb_l = axial_rope(qb_l, row, col), axial_rope(kb_l, row, col)
    yb_l = block_attention(qb_l, jnp.concatenate([kb_c, kb_l], axis=1),
                           jnp.concatenate([vb_c, vb_l], axis=1)).reshape(bsz, seq_len, GROUP_W)

    gc_c, xc_c = lru_features(pc_c, p['lru_conv_w'], p['lru_conv_b'])
    gc_l, xc_l = lru_features(pc_l, p['lru_conv_w'], p['lru_conv_b'])
    hz = jnp.zeros((bsz, LRU_WIDTH), jnp.float32)
    rc_f, hl_f = rglru_direction(xc_c, p['lru_w_r'][0], p['lru_b_r'][0], p['lru_w_i'][0], p['lru_b_i'][0], p['lru_lambda'][0], hz, False)
    rc_b, hl_b = rglru_direction(xc_c, p['lru_w_r'][1], p['lru_b_r'][1], p['lru_w_i'][1], p['lru_b_i'][1], p['lru_lambda'][1], hz, True)
    rl_f, _ = rglru_direction(xc_l, p['lru_w_r'][0], p['lru_b_r'][0], p['lru_w_i'][0], p['lru_b_i'][0], p['lru_lambda'][0], hl_f, False)
    rl_b, _ = rglru_direction(xc_l, p['lru_w_r'][1], p['lru_b_r'][1], p['lru_w_i'][1], p['lru_b_i'][1], p['lru_lambda'][1], hl_b, True)
    yc_l = (gc_l * (rl_f + rl_b)).astype(dtype)

    lam = (jnp.exp(jnp.sum(p['diff_lambda_q1'] * p['diff_lambda_k1']).astype(jnp.float32))
           - jnp.exp(jnp.sum(p['diff_lambda_q2'] * p['diff_lambda_k2']).astype(jnp.float32)) + lam_init)
    qd_c, kd_c, vd_c = diff_qkv(pd_c, p['diff_q_norm_g'], p['diff_k_norm_g'])
    qd_l, kd_l, vd_l = diff_qkv(pd_l, p['diff_q_norm_g'], p['diff_k_norm_g'])
    qd_l, kd_l = axial_rope(qd_l, row, col), axial_rope(kd_l, row, col)
    yd_l = diff_attention(qd_l, jnp.concatenate([kd_c, kd_l], axis=1), jnp.concatenate([vd_c, vd_l], axis=1),
                          lam, lam_init, p['diff_subln_g'])

    mix_l = jnp.concatenate([ya_l, yb_l, yc_l, yd_l], axis=-1) @ p['w_out']
    h_lat = h_lat + mod_l[2] * mix_l
    v_l = modulate(rms_norm(h_lat, p['norm2_g']), mod_l[3], mod_l[4])
    h_lat = h_lat + mod_l[5] * sq_relu_mlp(v_l, p['w_mlp1'], p['w_mlp2'])

    if update_ctx:
        yb_c = block_attention(qb_c, kb_c, vb_c).reshape(bsz, CTX_LEN, GROUP_W)
        yc_c = (gc_c * (rc_f + rc_b)).astype(dtype)
        yd_c = diff_attention(qd_c, kd_c, vd_c, lam, lam_init, p['diff_subln_g'])
        mix_c = jnp.concatenate([ya_c, yb_c, yc_c, yd_c], axis=-1) @ p['w_out']
        h_ctx = h_ctx + mod_c[2] * mix_c
        v_c = modulate(rms_norm(h_ctx, p['norm2_g']), mod_c[3], mod_c[4])
        h_ctx = h_ctx + mod_c[5] * sq_relu_mlp(v_c, p['w_mlp1'], p['w_mlp2'])
    return h_lat, h_ctx


def setup_inputs(seed: int = 0) -> dict:
    key = jax.random.key(seed)
    keys = iter(jax.random.split(key, 48))

    def normal(shape, scale):
        return scale * jax.random.normal(next(keys), shape, jnp.float32)

    def gain(shape):
        return 1.0 + normal(shape, 0.05)

    def uniform(shape, lo, hi):
        return jax.random.uniform(next(keys), shape, jnp.float32, lo, hi)

    L = DEPTH
    x = normal((BATCH, SEQ, D_MODEL), 1.0)
    c = normal((BATCH, D_MODEL), 1.0)
    ctx = normal((BATCH, CTX_LEN, D_MODEL), 1.0)
    c_ctx = normal((D_MODEL,), 1.0)
    w_mod = normal((L, D_MODEL, N_MOD * D_MODEL), 0.5 * D_MODEL ** -0.5)
    b_mod = normal((L, N_MOD * D_MODEL), 0.02)
    norm1_g = gain((L, D_MODEL))
    w_in = normal((L, D_MODEL, IN_COLS), D_MODEL ** -0.5)
    ssd_conv_w = normal((L, CONV_W, SSD_XBC), CONV_W ** -0.5)
    ssd_conv_b = normal((L, SSD_XBC), 0.02)
    ssd_a_log = jnp.log(uniform((L, 2, SSD_HEADS), 1.0, 16.0))
    dt0 = jnp.exp(uniform((L, 2, SSD_HEADS), math.log(1e-3), math.log(1e-1)))
    ssd_dt_bias = dt0 + jnp.log(-jnp.expm1(-dt0))
    ssd_d = gain((L, 2, SSD_HEADS))
    ssd_norm_g = gain((L, GROUP_W))
    gqa_q_norm_g = gain((L, GQA_HEAD_DIM))
    gqa_k_norm_g = gain((L, GQA_HEAD_DIM))
    lru_conv_w = normal((L, CONV_W, LRU_WIDTH), CONV_W ** -0.5)
    lru_conv_b = normal((L, LRU_WIDTH), 0.02)
    lru_w_r = normal((L, 2, LRU_BLOCKS, LRU_BLOCK_W, LRU_BLOCK_W), LRU_BLOCK_W ** -0.5)
    lru_b_r = normal((L, 2, LRU_WIDTH), 0.02)
    lru_w_i = normal((L, 2, LRU_BLOCKS, LRU_BLOCK_W, LRU_BLOCK_W), LRU_BLOCK_W ** -0.5)
    lru_b_i = normal((L, 2, LRU_WIDTH), 0.02)
    a_pow = uniform((L, 2, LRU_WIDTH), 0.9, 0.999)
    a_base = a_pow ** (1.0 / LRU_C)
    lru_lambda = jnp.log(a_base) - jnp.log1p(-a_base)
    diff_q_norm_g = gain((L, DIFF_QK_DIM))
    diff_k_norm_g = gain((L, DIFF_QK_DIM))
    diff_lambda_q1 = normal((L, DIFF_QK_DIM), 0.1)
    diff_lambda_k1 = normal((L, DIFF_QK_DIM), 0.1)
    diff_lambda_q2 = normal((L, DIFF_QK_DIM), 0.1)
    diff_lambda_k2 = normal((L, DIFF_QK_DIM), 0.1)
    diff_subln_g = gain((L, DIFF_V_DIM))
    w_out = normal((L, D_MIX, D_MODEL), D_MIX ** -0.5)
    norm2_g = gain((L, D_MODEL))
    w_mlp1 = normal((L, D_MODEL, D_FF), D_MODEL ** -0.5)
    w_mlp2 = normal((L, D_FF, D_MODEL), D_FF ** -0.5)
    return {'x': x, 'c': c, 'ctx': ctx, 'c_ctx': c_ctx, 'w_mod': w_mod, 'b_mod': b_mod, 'norm1_g': norm1_g,
            'w_in': w_in, 'ssd_conv_w': ssd_conv_w, 'ssd_conv_b': ssd_conv_b, 'ssd_a_log': ssd_a_log,
            'ssd_dt_bias': ssd_dt_bias, 'ssd_d': ssd_d, 'ssd_norm_g': ssd_norm_g, 'gqa_q_norm_g': gqa_q_norm_g,
            'gqa_k_norm_g': gqa_k_norm_g, 'lru_conv_w': lru_conv_w, 'lru_conv_b': lru_conv_b, 'lru_w_r': lru_w_r,
            'lru_b_r': lru_b_r, 'lru_w_i': lru_w_i, 'lru_b_i': lru_b_i, 'lru_lambda': lru_lambda,
            'diff_q_norm_g': diff_q_norm_g, 'diff_k_norm_g': diff_k_norm_g, 'diff_lambda_q1': diff_lambda_q1,
            'diff_lambda_k1': diff_lambda_k1, 'diff_lambda_q2': diff_lambda_q2, 'diff_lambda_k2': diff_lambda_k2,
            'diff_subln_g': diff_subln_g, 'w_out': w_out, 'norm2_g': norm2_g, 'w_mlp1': w_mlp1, 'w_mlp2': w_mlp2}


def reference(x, c, ctx, c_ctx, w_mod, b_mod, norm1_g, w_in, ssd_conv_w, ssd_conv_b, ssd_a_log, ssd_dt_bias,
              ssd_d, ssd_norm_g, gqa_q_norm_g, gqa_k_norm_g, lru_conv_w, lru_conv_b, lru_w_r, lru_b_r, lru_w_i,
              lru_b_i, lru_lambda, diff_q_norm_g, diff_k_norm_g, diff_lambda_q1, diff_lambda_k1, diff_lambda_q2,
              diff_lambda_k2, diff_subln_g, w_out, norm2_g, w_mlp1, w_mlp2):
    n_tok = x.shape[1]
    ROWS = n_tok // GRID_W
    row = jnp.repeat(jnp.arange(ROWS, dtype=jnp.int32), GRID_W)
    col = jnp.tile(jnp.arange(GRID_W, dtype=jnp.int32), ROWS)
    c_act = jax.nn.silu(c)
    cctx_act = jax.nn.silu(c_ctx)
    h_lat, h_ctx = x, ctx
    for l in range(DEPTH):
        p = {'w_mod': w_mod[l], 'b_mod': b_mod[l], 'norm1_g': norm1_g[l], 'w_in': w_in[l],
             'ssd_conv_w': ssd_conv_w[l], 'ssd_conv_b': ssd_conv_b[l], 'ssd_a_log': ssd_a_log[l],
             'ssd_dt_bias': ssd_dt_bias[l], 'ssd_d': ssd_d[l], 'ssd_norm_g': ssd_norm_g[l],
             'gqa_q_norm_g': gqa_q_norm_g[l], 'gqa_k_norm_g': gqa_k_norm_g[l],
             'lru_conv_w': lru_conv_w[l], 'lru_conv_b': lru_conv_b[l], 'lru_w_r': lru_w_r[l], 'lru_b_r': lru_b_r[l],
             'lru_w_i': lru_w_i[l], 'lru_b_i': lru_b_i[l], 'lru_lambda': lru_lambda[l],
             'diff_q_norm_g': diff_q_norm_g[l], 'diff_k_norm_g': diff_k_norm_g[l],
             'diff_lambda_q1': diff_lambda_q1[l], 'diff_lambda_k1': diff_lambda_k1[l],
             'diff_lambda_q2': diff_lambda_q2[l], 'diff_lambda_k2': diff_lambda_k2[l],
             'diff_subln_g': diff_subln_g[l], 'w_out': w_out[l], 'norm2_g': norm2_g[l],
             'w_mlp1': w_mlp1[l], 'w_mlp2': w_mlp2[l]}
        lam_init = 0.8 - 0.6 * math.exp(-0.3 * l)
        h_lat, h_ctx = trunk_layer(h_lat, h_ctx, c_act, cctx_act, p, row, col, lam_init, l < DEPTH - 1)
    return h_lat
```

```python
import functools
import math

import jax
import jax.numpy as jnp
import numpy as np
from jax import lax
from jax.experimental import pallas as pl
from jax.experimental.pallas import tpu as pltpu

F32 = jnp.float32
BF16 = jnp.bfloat16

D_MODEL = 1024
CTX = 256
GROUP_W = 256
D_FF = 4 * D_MODEL
N_MOD = 6
NORM_EPS = 1e-6
ROPE_THETA = 10000.0
GRID_W = 64
CONV_W = 4
CONV_PAD_L = CONV_W // 2
HALO = 8

SSD_HEADS = 4
SSD_P = 64
SSD_GROUPS = 2
SSD_N = 128
SSD_CHUNK = 128
SSD_XBC = GROUP_W + 2 * SSD_GROUPS * SSD_N
GQA_HEADS = 4
GQA_KV = 2
GQA_D = 64
GQA_COLS = GROUP_W + 2 * GQA_KV * GQA_D
LRU_C = 8.0
LRU_COLS = 2 * GROUP_W
DIFF_HEADS = 4
DIFF_V = 64
DIFF_QK = 32
DIFF_COLS = 3 * GROUP_W
DT_PAD = 128
IN_COLS_P = GROUP_W + SSD_XBC + GQA_COLS + LRU_COLS + DIFF_COLS + DT_PAD

ROW_TILE = 256
LRU_BLOCK = 256
ATT_TQ = 256
ATT_TK = 256
VMEM_LIMIT = 56 * 1024 * 1024


def _cparams(sem):
    return pltpu.CompilerParams(dimension_semantics=sem, vmem_limit_bytes=VMEM_LIMIT)


def _sigmoid(x):
    return 1.0 / (1.0 + jnp.exp(-x))


def _silu(x):
    return x * _sigmoid(x)


def _softplus(x):
    return jnp.maximum(x, 0.0) + jnp.log(1.0 + jnp.exp(-jnp.abs(x)))


def _rms(x, g):
    ms = jnp.mean(x * x, axis=-1, keepdims=True)
    return x * lax.rsqrt(ms + NORM_EPS) * g


def _dot(a, b):
    return jnp.dot(a, b, preferred_element_type=F32)


def _dot_nt(a, b):
    return lax.dot_general(a, b, (((1,), (1,)), ((), ())), preferred_element_type=F32)


def _dot_tn(a, b):
    return lax.dot_general(a, b, (((0,), (0,)), ((), ())), preferred_element_type=F32)


def _mod_kernel(c_ref, w_ref, b_ref, o_ref):
    act = _silu(c_ref[...]).astype(BF16)
    o_ref[0] = _dot(act, w_ref[0].astype(BF16)) + b_ref[0]


def _modulation(c8, w_mod, b_mod):
    depth = w_mod.shape[0]
    tn = 1536
    return pl.pallas_call(
        _mod_kernel,
        grid=(depth, (N_MOD * D_MODEL) // tn),
        in_specs=[pl.BlockSpec((8, D_MODEL), lambda l, j: (0, 0)),
                  pl.BlockSpec((1, D_MODEL, tn), lambda l, j: (l, 0, j)),
                  pl.BlockSpec((1, 1, tn), lambda l, j: (l, 0, j))],
        out_specs=pl.BlockSpec((1, 8, tn), lambda l, j: (l, 0, j)),
        out_shape=jax.ShapeDtypeStruct((depth, 8, N_MOD * D_MODEL), F32),
        compiler_params=_cparams(("arbitrary", "arbitrary")),
        name="modulation",
    )(c8, w_mod, b_mod.reshape(depth, 1, -1))


def _mod_row(i, tpb):
    b = i // tpb
    return jnp.where(i % tpb == tpb - 1, 2, b)


def _inproj_kernel(h_ref, mod_ref, g_ref, w_ref, z_ref, xbc_ref, gqa_ref, lru_ref, diff_ref, dt_ref):
    x = h_ref[...]
    u = _rms(x, g_ref[...]) * (1.0 + mod_ref[0, 1:2, :]) + mod_ref[0, 0:1, :]
    p = _dot(u.astype(BF16), w_ref[...])
    c0 = 0
    for ref in (z_ref, xbc_ref, gqa_ref, lru_ref, diff_ref, dt_ref):
        w = ref.shape[-1]
        ref[...] = p[:, c0:c0 + w]
        c0 += w


def _in_proj(h, mod3, g, w_in_p, tpb):
    n = h.shape[0]
    widths = (GROUP_W, SSD_XBC, GQA_COLS, LRU_COLS, DIFF_COLS, DT_PAD)
    return pl.pallas_call(
        _inproj_kernel,
        grid=(n // ROW_TILE,),
        in_specs=[pl.BlockSpec((ROW_TILE, D_MODEL), lambda i: (i, 0)),
                  pl.BlockSpec((1, N_MOD, D_MODEL), lambda i: (_mod_row(i, tpb), 0, 0)),
                  pl.BlockSpec((1, D_MODEL), lambda i: (0, 0)),
                  pl.BlockSpec((D_MODEL, IN_COLS_P), lambda i: (0, 0))],
        out_specs=[pl.BlockSpec((ROW_TILE, w), lambda i: (i, 0)) for w in widths],
        out_shape=[jax.ShapeDtypeStruct((n, w), F32) for w in widths],
        compiler_params=_cparams(("arbitrary",)),
        name="in_proj",
    )(h, mod3, g, w_in_p)


def _conv_block(ext_ref, x, xp, xn, w_ref, b_ref, rows):
    ext_ref[0:HALO, :] = xp
    ext_ref[HALO:HALO + rows, :] = x
    ext_ref[HALO + rows:2 * HALO + rows, :] = xn
    acc = b_ref[...] + w_ref[0:1, :] * ext_ref[pl.ds(HALO - CONV_PAD_L, rows), :]
    for j in range(1, CONV_W):
        acc = acc + w_ref[j:j + 1, :] * ext_ref[pl.ds(HALO - CONV_PAD_L + j, rows), :]
    return acc


def _halo_specs(nb, rows, width, col_blk, blk_of_step, t):
    per = rows // HALO
    last = t // HALO - 1
    prev = pl.BlockSpec((nb, HALO, width), lambda s: (0, jnp.maximum(blk_of_step(s) * per - 1, 0), col_blk))
    nxt = pl.BlockSpec((nb, HALO, width), lambda s: (0, jnp.minimum((blk_of_step(s) + 1) * per, last), col_blk))
    return prev, nxt


def _ssd_kernel(rev, n_lat_chunks, *refs):
    if rev:
        (xbc_ref, xp_ref, xn_ref, dt_ref, z_ref, yf_ref, cw_ref, cb_ref, dtb_ref, a_ref, d_ref, ng_ref,
         o_ref, h_scr, ext_scr) = refs
    else:
        (xbc_ref, xp_ref, xn_ref, dt_ref, cw_ref, cb_ref, dtb_ref, a_ref, d_ref,
         o_ref, h_scr, ext_scr) = refs
    nb = xbc_ref.shape[0]
    q = SSD_CHUNK
    s = pl.program_id(0)
    n_steps = pl.num_programs(0)
    if rev:
        chunk = n_steps - 1 - s
    else:
        chunk = jnp.where(s < 2, n_lat_chunks + s, s - 2)
    seg_start = (chunk == 0) | (chunk == n_lat_chunks)
    seg_end = (chunk == n_lat_chunks - 1) | (chunk == n_lat_chunks + 1)

    @pl.when(s == 0)
    def _():
        h_scr[...] = jnp.zeros_like(h_scr)

    li = lax.broadcasted_iota(jnp.int32, (q, q), 0)
    si = lax.broadcasted_iota(jnp.int32, (q, q), 1)
    keep = (li <= si) if rev else (li >= si)
    tri = keep.astype(F32)
    last = 0 if rev else q - 1
    col0 = SSD_HEADS if rev else 0

    for b in range(nb):
        xp = jnp.where(seg_start, 0.0, xp_ref[b])
        xn = jnp.where(seg_end, 0.0, xn_ref[b])
        xbc = _silu(_conv_block(ext_scr, xbc_ref[b], xp, xn, cw_ref, cb_ref, q))
        xs = xbc[:, 0:GROUP_W]
        sp = _softplus(dt_ref[b] + dtb_ref[...])
        da = sp * a_ref[...]
        acum = jnp.dot(tri, da, preferred_element_type=F32, precision=lax.Precision.HIGHEST)
        acum_t = acum.T
        ys = []
        for g in range(SSD_GROUPS):
            bm = xbc[:, GROUP_W + g * SSD_N:GROUP_W + (g + 1) * SSD_N]
            cm = xbc[:, GROUP_W + (SSD_GROUPS + g) * SSD_N:GROUP_W + (SSD_GROUPS + g + 1) * SSD_N]
            cb = _dot_nt(cm.astype(BF16), bm.astype(BF16))
            for hh in range(g * (SSD_HEADS // SSD_GROUPS), (g + 1) * (SSD_HEADS // SSD_GROUPS)):
                c = col0 + hh
                ac = acum[:, c:c + 1]
                ar = acum_t[c:c + 1, :]
                tot = acum[last:last + 1, c:c + 1]
                decay = jnp.exp(jnp.where(keep, ac - ar, -jnp.inf))
                xh = xs[:, hh * SSD_P:(hh + 1) * SSD_P]
                xdt = (xh * sp[:, c:c + 1]).astype(BF16)
                y = _dot((cb * decay).astype(BF16), xdt)
                hin = h_scr[b, hh]
                y = y + _dot((cm * jnp.exp(ac)).astype(BF16), hin.astype(BF16))
                bw = (bm * jnp.exp(tot - ac)).astype(BF16)
                h_scr[b, hh] = hin * jnp.exp(tot) + _dot_tn(bw, xdt)
                ys.append(y)
        y = jnp.concatenate(ys, axis=1) + d_ref[...] * xs
        if rev:
            y = (y + yf_ref[b]) * _silu(z_ref[b])
            half = GROUP_W // SSD_GROUPS
            y = jnp.concatenate([_rms(y[:, :half], ng_ref[:, :half]), _rms(y[:, half:], ng_ref[:, half:])], axis=1)
        o_ref[b] = y


def _ssd_direction(rev, xbc, dt, z, yf, cw, cb, dtb, a_row, d_row, ng, t):
    nb = xbc.shape[0]
    n_chunks = t // SSD_CHUNK
    n_lat_chunks = (t - CTX) // SSD_CHUNK

    def chunk_of(s):
        if rev:
            return n_chunks - 1 - s
        return jnp.where(s < 2, n_lat_chunks + s, s - 2)

    def tok(width):
        return pl.BlockSpec((nb, SSD_CHUNK, width), lambda s: (0, chunk_of(s), 0))

    def const(shape):
        return pl.BlockSpec(shape, lambda s: (0,) * len(shape))

    prev, nxt = _halo_specs(nb, SSD_CHUNK, SSD_XBC, 0, chunk_of, t)
    in_specs = [tok(SSD_XBC), prev, nxt, tok(DT_PAD)]
    args = [xbc, xbc, xbc, dt]
    if rev:
        in_specs += [tok(GROUP_W), tok(GROUP_W)]
        args += [z, yf]
    in_specs += [const((CONV_W, SSD_XBC)), const((1, SSD_XBC)), const((1, DT_PAD)), const((1, DT_PAD)),
                 const((1, GROUP_W))]
    args += [cw, cb, dtb, a_row, d_row]
    if rev:
        in_specs.append(const((1, GROUP_W)))
        args.append(ng)
    return pl.pallas_call(
        functools.partial(_ssd_kernel, rev, n_lat_chunks),
        grid=(n_chunks,),
        in_specs=in_specs,
        out_specs=tok(GROUP_W),
        out_shape=jax.ShapeDtypeStruct((nb, t, GROUP_W), F32),
        scratch_shapes=[pltpu.VMEM((nb, SSD_HEADS, SSD_N, SSD_P), F32),
                        pltpu.VMEM((SSD_CHUNK + 2 * HALO, SSD_XBC), F32)],
        compiler_params=_cparams(("arbitrary",)),
        name="ssd_bwd" if rev else "ssd_fwd",
    )(*args)


def _lru_kernel(rev, n_lat_blocks, *refs):
    if rev:
        (x_ref, xp_ref, xn_ref, gate_ref, hf_ref, cw_ref, cb_ref, w_ref, bias_ref, lam_ref,
         o_ref, h_scr, a_scr, u_scr, ho_scr, ext_scr) = refs
    else:
        (x_ref, xp_ref, xn_ref, cw_ref, cb_ref, w_ref, bias_ref, lam_ref,
         o_ref, h_scr, a_scr, u_scr, ext_scr) = refs
        ho_scr = o_ref
    nb = x_ref.shape[0]
    rows = LRU_BLOCK
    s = pl.program_id(0)
    n_steps = pl.num_programs(0)
    if rev:
        blk = n_steps - 1 - s
    else:
        blk = jnp.where(s == 0, n_lat_blocks, s - 1)
    seg_start = (blk == 0) | (blk == n_lat_blocks)
    seg_end = (blk == n_lat_blocks - 1) | (blk == n_lat_blocks)

    @pl.when(s == 0)
    def _():
        h_scr[...] = jnp.zeros_like(h_scr)

    sp_lam = _softplus(-lam_ref[...])
    for b in range(nb):
        xp = jnp.where(seg_start, 0.0, xp_ref[b])
        xn = jnp.where(seg_end, 0.0, xn_ref[b])
        xc = _conv_block(ext_scr, x_ref[b], xp, xn, cw_ref, cb_ref, rows)
        ri = _dot(xc.astype(BF16), w_ref[...]) + bias_ref[...]
        r = _sigmoid(ri[:, :GROUP_W])
        gi = _sigmoid(ri[:, GROUP_W:])
        log_a = -LRU_C * r * sp_lam
        a_scr[b] = jnp.exp(log_a)
        u_scr[b] = jnp.sqrt(1.0 - jnp.exp(2.0 * log_a)) * (gi * xc)

    def step(g8, hs):
        hs = list(hs)
        for k in range(8):
            row = g8 * 8 + k
            if rev:
                row = rows - 1 - row
            for b in range(nb):
                h = a_scr[b, pl.ds(row, 1), :] * hs[b] + u_scr[b, pl.ds(row, 1), :]
                ho_scr[b, pl.ds(row, 1), :] = h
                hs[b] = h
        return tuple(hs)

    hs = lax.fori_loop(0, rows // 8, step, tuple(h_scr[b] for b in range(nb)))
    for b in range(nb):
        h_scr[b] = hs[b]
    if rev:
        for b in range(nb):
            gt = gate_ref[b]
            gelu = 0.5 * gt * (1.0 + jnp.tanh(math.sqrt(2.0 / math.pi) * (gt + 0.044715 * gt * gt * gt)))
            o_ref[b] = gelu * (hf_ref[b] + ho_scr[b])


def _lru_direction(rev, lru, hf, cw, cb, w_dir, bias_dir, lam_dir, t):
    nb = lru.shape[0]
    n_blocks = t // LRU_BLOCK
    n_lat_blocks = (t - CTX) // LRU_BLOCK

    def blk_of(s):
        if rev:
            return n_blocks - 1 - s
        return jnp.where(s == 0, n_lat_blocks, s - 1)

    def tok(col_blk):
        return pl.BlockSpec((nb, LRU_BLOCK, GROUP_W), lambda s: (0, blk_of(s), col_blk))

    def const(shape):
        return pl.BlockSpec(shape, lambda s: (0,) * len(shape))

    prev, nxt = _halo_specs(nb, LRU_BLOCK, GROUP_W, 1, blk_of, t)
    in_specs = [tok(1), prev, nxt]
    args = [lru, lru, lru]
    if rev:
        in_specs += [tok(0), tok(0)]
        args += [lru, hf]
    in_specs += [const((CONV_W, GROUP_W)), const((1, GROUP_W)), const((GROUP_W, 2 * GROUP_W)),
                 const((1, 2 * GROUP_W)), const((1, GROUP_W))]
    args += [cw, cb, w_dir, bias_dir, lam_dir]
    scratch = [pltpu.VMEM((nb, 1, GROUP_W), F32), pltpu.VMEM((nb, LRU_BLOCK, GROUP_W), F32),
               pltpu.VMEM((nb, LRU_BLOCK, GROUP_W), F32)]
    if rev:
        scratch.append(pltpu.VMEM((nb, LRU_BLOCK, GROUP_W), F32))
    scratch.append(pltpu.VMEM((LRU_BLOCK + 2 * HALO, GROUP_W), F32))
    return pl.pallas_call(
        functools.partial(_lru_kernel, rev, n_lat_blocks),
        grid=(n_blocks,),
        in_specs=in_specs,
        out_specs=pl.BlockSpec((nb, LRU_BLOCK, GROUP_W), lambda s: (0, blk_of(s), 0)),
        out_shape=jax.ShapeDtypeStruct((nb, t, GROUP_W), F32),
        scratch_shapes=scratch,
        compiler_params=_cparams(("arbitrary",)),
        name="lru_bwd" if rev else "lru_fwd",
    )(*args)


def _seg_mean(x2, seg_ref):
    return jnp.dot(x2, seg_ref[...], preferred_element_type=F32, precision=lax.Precision.HIGHEST)


def _rope(x, cos, sin_lo, sin_hi, half):
    w = x.shape[-1]
    rep = w // cos.shape[-1]
    cos, sin_lo, sin_hi = (jnp.concatenate([t] * rep, axis=1) if rep > 1 else t for t in (cos, sin_lo, sin_hi))
    return x * cos + pltpu.roll(x, w - half, 1) * sin_lo + pltpu.roll(x, half, 1) * sin_hi


def _prep_kernel(gqa_ref, diff_ref, gtab_ref, dtab_ref, seg64_ref, seg32_ref, gq_g_ref, gk_g_ref,
                 dq_g_ref, dk_g_ref, gq_ref, gk_ref, gv_ref, dq_ref, dk_ref, dv_ref):
    def norm_rope(x, seg_ref, g, tab_ref, half, scale):
        w = x.shape[-1]
        xn = x * lax.rsqrt(_seg_mean(x * x, seg_ref)[:, :w] + NORM_EPS) * g[:, :w]
        xr = _rope(xn, tab_ref[0], tab_ref[1], tab_ref[2], half)
        return xr * scale if scale != 1.0 else xr

    def scatter(ref, x, n, d):
        for hh in range(n):
            ref[0, hh] = x[:, hh * d:(hh + 1) * d].astype(ref.dtype)

    gqa = gqa_ref[...]
    kw = GQA_KV * GQA_D
    q = norm_rope(gqa[:, :GROUP_W], seg64_ref, gq_g_ref[...], gtab_ref, GQA_D // 4, GQA_D ** -0.5)
    k = _pad_lanes_rope(gqa[:, GROUP_W:GROUP_W + kw], seg64_ref, gk_g_ref[...], gtab_ref, GQA_D // 4)
    scatter(gq_ref, q, GQA_HEADS, GQA_D)
    scatter(gk_ref, k, GQA_KV, GQA_D)
    scatter(gv_ref, gqa[:, GROUP_W + kw:], GQA_KV, GQA_D)
    diff = diff_ref[...]
    dq = norm_rope(diff[:, :GROUP_W], seg32_ref, dq_g_ref[...], dtab_ref, DIFF_QK // 4, DIFF_QK ** -0.5)
    dk = norm_rope(diff[:, GROUP_W:2 * GROUP_W], seg32_ref, dk_g_ref[...], dtab_ref, DIFF_QK // 4, 1.0)
    scatter(dq_ref, dq, 2 * DIFF_HEADS, DIFF_QK)
    scatter(dk_ref, dk, 2 * DIFF_HEADS, DIFF_QK)
    scatter(dv_ref, diff[:, 2 * GROUP_W:], DIFF_HEADS, DIFF_V)


def _pad_lanes_rope(x, seg_ref, g, tab_ref, half):
    w = x.shape[-1]
    xn = x * lax.rsqrt(jnp.dot(x * x, seg_ref[:w, :w], preferred_element_type=F32,
                               precision=lax.Precision.HIGHEST) + NORM_EPS) * g[:, :w]
    return _rope(xn, tab_ref[0], tab_ref[1], tab_ref[2], half)


def _attn_prep(gqa, diff, gtab, dtab, seg64, seg32, gq_g, gk_g, dq_g, dk_g, nb, t):
    tpb = t // ROW_TILE

    def heads(n, d):
        return (pl.BlockSpec((1, n, ROW_TILE, d), lambda i: (i // tpb, 0, i % tpb, 0)),
                jax.ShapeDtypeStruct((nb, n, t, d), BF16))

    outs = [heads(GQA_HEADS, GQA_D), heads(GQA_KV, GQA_D), heads(GQA_KV, GQA_D),
            heads(2 * DIFF_HEADS, DIFF_QK), heads(2 * DIFF_HEADS, DIFF_QK), heads(DIFF_HEADS, DIFF_V)]

    def const(shape):
        return pl.BlockSpec(shape, lambda i: (0,) * len(shape))

    return pl.pallas_call(
        _prep_kernel,
        grid=(nb * tpb,),
        in_specs=[pl.BlockSpec((ROW_TILE, GQA_COLS), lambda i: (i, 0)),
                  pl.BlockSpec((ROW_TILE, DIFF_COLS), lambda i: (i, 0)),
                  pl.BlockSpec((3, ROW_TILE, 128), lambda i: (0, i % tpb, 0)),
                  pl.BlockSpec((3, ROW_TILE, 128), lambda i: (0, i % tpb, 0)),
                  const((GROUP_W, GROUP_W)), const((GROUP_W, GROUP_W)),
                  const((1, GROUP_W)), const((1, GROUP_W)), const((1, GROUP_W)), const((1, GROUP_W))],
        out_specs=[o[0] for o in outs],
        out_shape=[o[1] for o in outs],
        compiler_params=_cparams(("arbitrary",)),
        name="attn_prep",
    )(gqa, diff, gtab, dtab, seg64, seg32, gq_g, gk_g, dq_g, dk_g)


def _attn_kernel(diff_mode, n_units, post_scale, q_ref, k_ref, v_ref, *rest):
    if diff_mode:
        lam_ref, sg_ref, o_ref, m_scr, l_scr, acc_scr = rest
    else:
        o_ref, m_scr, l_scr, acc_scr = rest
    tq, tk = ATT_TQ, ATT_TK
    qi = pl.program_id(2)
    n_q = pl.num_programs(2)
    n_chunks = k_ref.shape[2] // tk
    lo = jnp.where(qi == n_q - 1, n_chunks - 1, 0)

    m_scr[...] = jnp.full_like(m_scr, -jnp.inf)
    l_scr[...] = jnp.zeros_like(l_scr)
    acc_scr[...] = jnp.zeros_like(acc_scr)

    def body(c, carry):
        off = pl.multiple_of(c * tk, tk)
        for u in range(n_units):
            if diff_mode:
                s = jnp.concatenate(
                    [_dot_nt(q_ref[0, 2 * u + j], k_ref[0, 2 * u + j, pl.ds(off, tk), :]) for j in range(2)], axis=0)
            else:
                s = _dot_nt(q_ref[0].reshape(2 * tq, q_ref.shape[-1]), k_ref[0, u, pl.ds(off, tk), :])
            m_old = m_scr[u]
            m_new = jnp.maximum(m_old, jnp.max(s, axis=-1, keepdims=True))
            alpha = jnp.exp(m_old - m_new)
            p = jnp.exp(s - m_new)
            l_scr[u] = alpha * l_scr[u] + jnp.sum(p, axis=-1, keepdims=True)
            acc_scr[u] = alpha * acc_scr[u] + _dot(p.astype(BF16), v_ref[0, u, pl.ds(off, tk), :])
            m_scr[u] = m_new
        return carry

    lax.fori_loop(lo, n_chunks, body, 0)

    outs = []
    for u in range(n_units):
        o = acc_scr[u] / l_scr[u]
        if diff_mode:
            o = o[:tq] - lam_ref[:, :o.shape[-1]] * o[tq:]
            outs.append(_rms(o, sg_ref[...]) * post_scale)
        else:
            outs += [o[:tq], o[tq:]]
    o_ref[0] = jnp.concatenate(outs, axis=1)


def _attention(diff_mode, q, k, v, lam_row, subln_g, post_scale):
    nb, _, t, dqk = q.shape
    dv = v.shape[-1]
    n_units = 2 if diff_mode else 1
    n_groups = v.shape[1] // n_units
    qh = q.shape[1] // n_groups
    kh = k.shape[1] // n_groups
    in_specs = [pl.BlockSpec((1, qh, ATT_TQ, dqk), lambda b, g, i: (b, g, i, 0)),
                pl.BlockSpec((1, kh, t, dqk), lambda b, g, i: (b, g, 0, 0)),
                pl.BlockSpec((1, n_units, t, dv), lambda b, g, i: (b, g, 0, 0))]
    args = [q, k, v]
    if diff_mode:
        in_specs += [pl.BlockSpec((1, 128), lambda b, g, i: (0, 0)), pl.BlockSpec((1, dv), lambda b, g, i: (0, 0))]
        args += [lam_row, subln_g]
    return pl.pallas_call(
        functools.partial(_attn_kernel, diff_mode, n_units, post_scale),
        grid=(nb, n_groups, t // ATT_TQ),
        in_specs=in_specs,
        out_specs=pl.BlockSpec((1, ATT_TQ, 128), lambda b, g, i: (b, i, g)),
        out_shape=jax.ShapeDtypeStruct((nb, t, GROUP_W), F32),
        scratch_shapes=[pltpu.VMEM((n_units, 2 * ATT_TQ, 1), F32), pltpu.VMEM((n_units, 2 * ATT_TQ, 1), F32),
                        pltpu.VMEM((n_units, 2 * ATT_TQ, dv), F32)],
        compiler_params=_cparams(("arbitrary", "arbitrary", "arbitrary")),
        name="diff_attn" if diff_mode else "gqa_attn",
    )(*args)


def _outmlp_kernel(h_ref, ya_ref, yb_ref, yc_ref, yd_ref, mod_ref, g_ref, wo_ref, w1_ref, w2_ref, o_ref):
    mix = jnp.concatenate([ya_ref[...], yb_ref[...], yc_ref[...], yd_ref[...]], axis=1).astype(BF16)
    h1 = h_ref[...] + mod_ref[0, 2:3, :] * _dot(mix, wo_ref[...])
    v = _rms(h1, g_ref[...]) * (1.0 + mod_ref[0, 4:5, :]) + mod_ref[0, 3:4, :]
    u = jnp.maximum(_dot(v.astype(BF16), w1_ref[...]), 0.0)
    o_ref[...] = h1 + mod_ref[0, 5:6, :] * _dot((u * u).astype(BF16), w2_ref[...])


def _out_mlp(h, ya, yb, yc, yd, mod3, g2, w_out, w1, w2, tpb):
    n = h.shape[0]

    def tok(w):
        return pl.BlockSpec((ROW_TILE, w), lambda i: (i, 0))

    def const(shape):
        return pl.BlockSpec(shape, lambda i: (0, 0), pipeline_mode=pl.Buffered(1))

    return pl.pallas_call(
        _outmlp_kernel,
        grid=(n // ROW_TILE,),
        in_specs=[tok(D_MODEL), tok(GROUP_W), tok(GROUP_W), tok(GROUP_W), tok(GROUP_W),
                  pl.BlockSpec((1, N_MOD, D_MODEL), lambda i: (_mod_row(i, tpb), 0, 0)),
                  const((1, D_MODEL)), const((D_MODEL, D_MODEL)), const((D_MODEL, D_FF)), const((D_FF, D_MODEL))],
        out_specs=tok(D_MODEL),
        out_shape=jax.ShapeDtypeStruct((n, D_MODEL), F32),
        compiler_params=_cparams(("arbitrary",)),
        name="out_mlp",
    )(h, ya, yb, yc, yd, mod3, g2, w_out, w1, w2)


def _rope_tables(s_len, head_dim, lanes=128):
    m = head_dim // 2
    half = m // 2
    lane = np.arange(lanes)
    d = lane % head_dim
    freq = ROPE_THETA ** (-(d % half).astype(np.float64) / half)
    t = np.arange(s_len)
    pos = np.where((d < m)[None, :], (t // GRID_W)[:, None], (t % GRID_W)[:, None]).astype(np.float32)
    ang = pos * freq.astype(np.float32)[None, :]
    cos, sin = np.cos(ang), np.sin(ang)
    low = ((d % m) < half)[None, :]
    tab = np.stack([cos, np.where(low, -sin, 0.0), np.where(low, 0.0, sin)])
    ident = np.stack([np.ones((CTX, lanes)), np.zeros((CTX, lanes)), np.zeros((CTX, lanes))])
    return jnp.asarray(np.concatenate([tab, ident], axis=1), F32)


def _seg_matrix(width, seg):
    idx = np.arange(width) // seg
    return jnp.asarray((idx[:, None] == idx[None, :]).astype(np.float32) / seg)


def _lane_tile(v, width):
    return jnp.tile(v, width // v.shape[-1]).reshape(1, width)


def _block_diag(w):
    nblk, j, k = w.shape
    eye = jnp.eye(nblk, dtype=w.dtype)
    return jnp.einsum('njk,nm->njmk', w, eye).reshape(nblk * j, nblk * k)


def kernel(x, c, ctx, c_ctx, w_mod, b_mod, norm1_g, w_in, ssd_conv_w, ssd_conv_b, ssd_a_log, ssd_dt_bias, ssd_d, ssd_norm_g, gqa_q_norm_g, gqa_k_norm_g, lru_conv_w, lru_conv_b, lru_w_r, lru_b_r, lru_w_i, lru_b_i, lru_lambda, diff_q_norm_g, diff_k_norm_g, diff_lambda_q1, diff_lambda_k1, diff_lambda_q2, diff_lambda_k2, diff_subln_g, w_out, norm2_g, w_mlp1, w_mlp2):
    nb, s_len, d_model = x.shape
    depth = w_mod.shape[0]
    assert d_model == D_MODEL and ctx.shape[1] == CTX and s_len % ROW_TILE == 0 and s_len % GRID_W == 0
    t = s_len + CTX
    tpb = t // ROW_TILE
    n = nb * t

    ssd_cols = GROUP_W + SSD_XBC + 2 * SSD_HEADS
    w_in_p = jnp.concatenate(
        [w_in[:, :, :GROUP_W + SSD_XBC], w_in[:, :, ssd_cols:], w_in[:, :, GROUP_W + SSD_XBC:ssd_cols],
         jnp.zeros((depth, D_MODEL, DT_PAD - 2 * SSD_HEADS), w_in.dtype)], axis=-1).astype(BF16)
    w_out_b, w1_b, w2_b = w_out.astype(BF16), w_mlp1.astype(BF16), w_mlp2.astype(BF16)
    lru_w = jnp.stack([jnp.concatenate([_block_diag(lru_w_r[l, d]), _block_diag(lru_w_i[l, d])], axis=1)
                       for l in range(depth) for d in range(2)]).reshape(depth, 2, GROUP_W, 2 * GROUP_W).astype(BF16)
    lru_b = jnp.concatenate([lru_b_r, lru_b_i], axis=-1)
    pad8 = DT_PAD - 2 * SSD_HEADS
    dtb_rows = jnp.pad(ssd_dt_bias.reshape(depth, 1, 2 * SSD_HEADS), ((0, 0), (0, 0), (0, pad8)))
    a_rows = jnp.pad(-jnp.exp(ssd_a_log.reshape(depth, 1, 2 * SSD_HEADS)), ((0, 0), (0, 0), (0, pad8)))
    d_rows = jnp.repeat(ssd_d, SSD_P, axis=-1)
    lam_diff = (jnp.exp(jnp.sum(diff_lambda_q1 * diff_lambda_k1, axis=-1))
                - jnp.exp(jnp.sum(diff_lambda_q2 * diff_lambda_k2, axis=-1)))
    gtab = _rope_tables(s_len, GQA_D)
    dtab = _rope_tables(s_len, DIFF_QK)
    seg64 = _seg_matrix(GROUP_W, GQA_D)
    seg32 = _seg_matrix(GROUP_W, DIFF_QK)

    c8 = jnp.concatenate([c, c_ctx[None, :], jnp.zeros((8 - nb - 1, D_MODEL), F32)], axis=0)
    mod = _modulation(c8, w_mod, b_mod)
    h = jnp.concatenate([x, ctx], axis=1).reshape(n, D_MODEL)

    for l in range(depth):
        lam_init = 0.8 - 0.6 * math.exp(-0.3 * l)
        mod3 = mod[l].reshape(8, N_MOD, D_MODEL)
        z, xbc, gqa, lru, diff, dt = _in_proj(h, mod3, norm1_g[l].reshape(1, -1), w_in_p[l], tpb)
        r3 = lambda a: a.reshape(nb, t, a.shape[-1])
        ssd_args = (r3(xbc), r3(dt), r3(z))
        cw, cb = ssd_conv_w[l], ssd_conv_b[l].reshape(1, -1)
        ya_f = _ssd_direction(False, *ssd_args, None, cw, cb, dtb_rows[l], a_rows[l], d_rows[l, 0:1], None, t)
        ya = _ssd_direction(True, *ssd_args, ya_f, cw, cb, dtb_rows[l], a_rows[l], d_rows[l, 1:2],
                            ssd_norm_g[l].reshape(1, -1), t)
        lcw, lcb = lru_conv_w[l], lru_conv_b[l].reshape(1, -1)
        hf = _lru_direction(False, r3(lru), None, lcw, lcb, lru_w[l, 0], lru_b[l, 0:1], lru_lambda[l, 0:1], t)
        yc = _lru_direction(True, r3(lru), hf, lcw, lcb, lru_w[l, 1], lru_b[l, 1:2], lru_lambda[l, 1:2], t)
        gq, gk, gv, dq, dk, dv = _attn_prep(
            gqa, diff, gtab, dtab, seg64, seg32, _lane_tile(gqa_q_norm_g[l], GROUP_W),
            _lane_tile(gqa_k_norm_g[l], GROUP_W), _lane_tile(diff_q_norm_g[l], GROUP_W),
            _lane_tile(diff_k_norm_g[l], GROUP_W), nb, t)
        yb = _attention(False, gq, gk, gv, None, None, 1.0)
        lam_row = jnp.full((1, 128), lam_init, F32) + lam_diff[l]
        yd = _attention(True, dq, dk, dv, lam_row, diff_subln_g[l].reshape(1, -1), 1.0 - lam_init)
        h = _out_mlp(h, ya.reshape(n, -1), yb.reshape(n, -1), yc.reshape(n, -1), yd.reshape(n, -1), mod3,
                     norm2_g[l].reshape(1, -1), w_out_b[l], w1_b[l], w2_b[l], tpb)
    return h.reshape(nb, t, D_MODEL)[:, :s_len]
```

```python
import functools
import math

import jax
import jax.numpy as jnp
import numpy as np
from jax import lax
from jax.experimental import pallas as pl
from jax.experimental.pallas import tpu as pltpu

F32 = jnp.float32
BF16 = jnp.bfloat16

D_MODEL = 1024
CTX = 256
GROUP_W = 256
D_FF = 4 * D_MODEL
N_MOD = 6
NORM_EPS = 1e-6
ROPE_THETA = 10000.0
GRID_W = 64
CONV_W = 4
CONV_PAD_L = CONV_W // 2
HALO = 8

SSD_HEADS = 4
SSD_P = 64
SSD_GROUPS = 2
SSD_N = 128
SSD_CHUNK = 128
SSD_XBC = GROUP_W + 2 * SSD_GROUPS * SSD_N
GQA_HEADS = 4
GQA_KV = 2
GQA_D = 64
GQA_COLS = GROUP_W + 2 * GQA_KV * GQA_D
LRU_C = 8.0
LRU_COLS = 2 * GROUP_W
DIFF_HEADS = 4
DIFF_V = 64
DIFF_QK = 32
DIFF_COLS = 3 * GROUP_W
DT_PAD = 128
IN_COLS_P = GROUP_W + SSD_XBC + GQA_COLS + LRU_COLS + DIFF_COLS + DT_PAD

ROW_TILE = 256
LRU_BLOCK = 256
ATT_TQ = 256
KEY_TILE = 256
ATT_MAX_UNROLL = 11
VMEM_LIMIT = 56 * 1024 * 1024


def _cparams(sem):
    return pltpu.CompilerParams(dimension_semantics=sem, vmem_limit_bytes=VMEM_LIMIT)


def _sigmoid(x):
    return 1.0 / (1.0 + jnp.exp(-x))


def _silu(x):
    return x * _sigmoid(x)


def _softplus(x):
    return jnp.maximum(x, 0.0) + jnp.log(1.0 + jnp.exp(-jnp.abs(x)))


def _rms(x, g):
    ms = jnp.mean(x * x, axis=-1, keepdims=True)
    return x * lax.rsqrt(ms + NORM_EPS) * g


def _dot(a, b):
    return jnp.dot(a, b, preferred_element_type=F32)


def _dot_nt(a, b):
    return lax.dot_general(a, b, (((1,), (1,)), ((), ())), preferred_element_type=F32)


def _dot_tn(a, b):
    return lax.dot_general(a, b, (((0,), (0,)), ((), ())), preferred_element_type=F32)


def _mod_kernel(c_ref, w_ref, b_ref, o_ref):
    act = _silu(c_ref[...]).astype(BF16)
    o_ref[0] = _dot(act, w_ref[0].astype(BF16)) + b_ref[0]


def _modulation(c8, w_mod, b_mod):
    depth = w_mod.shape[0]
    tn = 1536
    return pl.pallas_call(
        _mod_kernel,
        grid=(depth, (N_MOD * D_MODEL) // tn),
        in_specs=[pl.BlockSpec((8, D_MODEL), lambda l, j: (0, 0)),
                  pl.BlockSpec((1, D_MODEL, tn), lambda l, j: (l, 0, j)),
                  pl.BlockSpec((1, 1, tn), lambda l, j: (l, 0, j))],
        out_specs=pl.BlockSpec((1, 8, tn), lambda l, j: (l, 0, j)),
        out_shape=jax.ShapeDtypeStruct((depth, 8, N_MOD * D_MODEL), F32),
        compiler_params=_cparams(("arbitrary", "arbitrary")),
        name="modulation",
    )(c8, w_mod, b_mod.reshape(depth, 1, -1))


def _mod_row(i, tpb):
    b = i // tpb
    return jnp.where(i % tpb == tpb - 1, 2, b)


def _inproj_kernel(h_ref, mod_ref, g_ref, w_ref, z_ref, xbc_ref, gqa_ref, lru_ref, diff_ref, dt_ref):
    x = h_ref[...]
    u = _rms(x, g_ref[...]) * (1.0 + mod_ref[0, 1:2, :]) + mod_ref[0, 0:1, :]
    p = _dot(u.astype(BF16), w_ref[...])
    c0 = 0
    for ref in (z_ref, xbc_ref, gqa_ref, lru_ref, diff_ref, dt_ref):
        w = ref.shape[-1]
        ref[...] = p[:, c0:c0 + w]
        c0 += w


def _in_proj(h, mod3, g, w_in_p, tpb):
    n = h.shape[0]
    widths = (GROUP_W, SSD_XBC, GQA_COLS, LRU_COLS, DIFF_COLS, DT_PAD)
    return pl.pallas_call(
        _inproj_kernel,
        grid=(n // ROW_TILE,),
        in_specs=[pl.BlockSpec((ROW_TILE, D_MODEL), lambda i: (i, 0)),
                  pl.BlockSpec((1, N_MOD, D_MODEL), lambda i: (_mod_row(i, tpb), 0, 0)),
                  pl.BlockSpec((1, D_MODEL), lambda i: (0, 0)),
                  pl.BlockSpec((D_MODEL, IN_COLS_P), lambda i: (0, 0))],
        out_specs=[pl.BlockSpec((ROW_TILE, w), lambda i: (i, 0)) for w in widths],
        out_shape=[jax.ShapeDtypeStruct((n, w), F32) for w in widths],
        compiler_params=_cparams(("arbitrary",)),
        name="in_proj",
    )(h, mod3, g, w_in_p)


def _conv_block(ext_ref, x, xp, xn, w_ref, b_ref, rows):
    ext_ref[0:HALO, :] = xp
    ext_ref[HALO:HALO + rows, :] = x
    ext_ref[HALO + rows:2 * HALO + rows, :] = xn
    acc = b_ref[...] + w_ref[0:1, :] * ext_ref[pl.ds(HALO - CONV_PAD_L, rows), :]
    for j in range(1, CONV_W):
        acc = acc + w_ref[j:j + 1, :] * ext_ref[pl.ds(HALO - CONV_PAD_L + j, rows), :]
    return acc


def _halo_specs(nb, rows, width, col_blk, blk_of_step, t):
    per = rows // HALO
    last = t // HALO - 1
    prev = pl.BlockSpec((nb, HALO, width), lambda s: (0, jnp.maximum(blk_of_step(s) * per - 1, 0), col_blk))
    nxt = pl.BlockSpec((nb, HALO, width), lambda s: (0, jnp.minimum((blk_of_step(s) + 1) * per, last), col_blk))
    return prev, nxt


def _ssd_kernel(rev, n_lat_chunks, *refs):
    if rev:
        (xbc_ref, xp_ref, xn_ref, dt_ref, z_ref, yf_ref, cw_ref, cb_ref, dtb_ref, a_ref, d_ref, ng_ref,
         o_ref, h_scr, ext_scr) = refs
    else:
        (xbc_ref, xp_ref, xn_ref, dt_ref, cw_ref, cb_ref, dtb_ref, a_ref, d_ref,
         o_ref, h_scr, ext_scr) = refs
    nb = xbc_ref.shape[0]
    q = SSD_CHUNK
    s = pl.program_id(0)
    n_steps = pl.num_programs(0)
    if rev:
        chunk = n_steps - 1 - s
    else:
        chunk = jnp.where(s < 2, n_lat_chunks + s, s - 2)
    seg_start = (chunk == 0) | (chunk == n_lat_chunks)
    seg_end = (chunk == n_lat_chunks - 1) | (chunk == n_lat_chunks + 1)

    @pl.when(s == 0)
    def _():
        h_scr[...] = jnp.zeros_like(h_scr)

    li = lax.broadcasted_iota(jnp.int32, (q, q), 0)
    si = lax.broadcasted_iota(jnp.int32, (q, q), 1)
    keep = (li <= si) if rev else (li >= si)
    tri = keep.astype(F32)
    last = 0 if rev else q - 1
    col0 = SSD_HEADS if rev else 0

    for b in range(nb):
        xp = jnp.where(seg_start, 0.0, xp_ref[b])
        xn = jnp.where(seg_end, 0.0, xn_ref[b])
        xbc = _silu(_conv_block(ext_scr, xbc_ref[b], xp, xn, cw_ref, cb_ref, q))
        xs = xbc[:, 0:GROUP_W]
        sp = _softplus(dt_ref[b] + dtb_ref[...])
        da = sp * a_ref[...]
        acum = jnp.dot(tri, da, preferred_element_type=F32, precision=lax.Precision.HIGHEST)
        acum_t = acum.T
        ys = []
        for g in range(SSD_GROUPS):
            bm = xbc[:, GROUP_W + g * SSD_N:GROUP_W + (g + 1) * SSD_N]
            cm = xbc[:, GROUP_W + (SSD_GROUPS + g) * SSD_N:GROUP_W + (SSD_GROUPS + g + 1) * SSD_N]
            cb = _dot_nt(cm.astype(BF16), bm.astype(BF16))
            for hh in range(g * (SSD_HEADS // SSD_GROUPS), (g + 1) * (SSD_HEADS // SSD_GROUPS)):
                c = col0 + hh
                ac = acum[:, c:c + 1]
                ar = acum_t[c:c + 1, :]
                tot = acum[last:last + 1, c:c + 1]
                decay = jnp.exp(jnp.where(keep, ac - ar, -jnp.inf))
                xh = xs[:, hh * SSD_P:(hh + 1) * SSD_P]
                xdt = (xh * sp[:, c:c + 1]).astype(BF16)
                y = _dot((cb * decay).astype(BF16), xdt)
                hin = h_scr[b, hh]
                y = y + _dot((cm * jnp.exp(ac)).astype(BF16), hin.astype(BF16))
                bw = (bm * jnp.exp(tot - ac)).astype(BF16)
                h_scr[b, hh] = hin * jnp.exp(tot) + _dot_tn(bw, xdt)
                ys.append(y)
        y = jnp.concatenate(ys, axis=1) + d_ref[...] * xs
        if rev:
            y = (y + yf_ref[b]) * _silu(z_ref[b])
            half = GROUP_W // SSD_GROUPS
            y = jnp.concatenate([_rms(y[:, :half], ng_ref[:, :half]), _rms(y[:, half:], ng_ref[:, half:])], axis=1)
        o_ref[b] = y


def _ssd_direction(rev, xbc, dt, z, yf, cw, cb, dtb, a_row, d_row, ng, t):
    nb = xbc.shape[0]
    n_chunks = t // SSD_CHUNK
    n_lat_chunks = (t - CTX) // SSD_CHUNK

    def chunk_of(s):
        if rev:
            return n_chunks - 1 - s
        return jnp.where(s < 2, n_lat_chunks + s, s - 2)

    def tok(width):
        return pl.BlockSpec((nb, SSD_CHUNK, width), lambda s: (0, chunk_of(s), 0))

    def const(shape):
        return pl.BlockSpec(shape, lambda s: (0,) * len(shape))

    prev, nxt = _halo_specs(nb, SSD_CHUNK, SSD_XBC, 0, chunk_of, t)
    in_specs = [tok(SSD_XBC), prev, nxt, tok(DT_PAD)]
    args = [xbc, xbc, xbc, dt]
    if rev:
        in_specs += [tok(GROUP_W), tok(GROUP_W)]
        args += [z, yf]
    in_specs += [const((CONV_W, SSD_XBC)), const((1, SSD_XBC)), const((1, DT_PAD)), const((1, DT_PAD)),
                 const((1, GROUP_W))]
    args += [cw, cb, dtb, a_row, d_row]
    if rev:
        in_specs.append(const((1, GROUP_W)))
        args.append(ng)
    return pl.pallas_call(
        functools.partial(_ssd_kernel, rev, n_lat_chunks),
        grid=(n_chunks,),
        in_specs=in_specs,
        out_specs=tok(GROUP_W),
        out_shape=jax.ShapeDtypeStruct((nb, t, GROUP_W), F32),
        scratch_shapes=[pltpu.VMEM((nb, SSD_HEADS, SSD_N, SSD_P), F32),
                        pltpu.VMEM((SSD_CHUNK + 2 * HALO, SSD_XBC), F32)],
        compiler_params=_cparams(("arbitrary",)),
        name="ssd_bwd" if rev else "ssd_fwd",
    )(*args)


def _lru_kernel(rev, n_lat_blocks, *refs):
    if rev:
        (x_ref, xp_ref, xn_ref, gate_ref, hf_ref, cw_ref, cb_ref, w_ref, bias_ref, lam_ref,
         o_ref, h_scr, a_scr, u_scr, ho_scr, ext_scr) = refs
    else:
        (x_ref, xp_ref, xn_ref, cw_ref, cb_ref, w_ref, bias_ref, lam_ref,
         o_ref, h_scr, a_scr, u_scr, ext_scr) = refs
        ho_scr = o_ref
    nb = x_ref.shape[0]
    rows = LRU_BLOCK
    s = pl.program_id(0)
    n_steps = pl.num_programs(0)
    if rev:
        blk = n_steps - 1 - s
    else:
        blk = jnp.where(s == 0, n_lat_blocks, s - 1)
    seg_start = (blk == 0) | (blk == n_lat_blocks)
    seg_end = (blk == n_lat_blocks - 1) | (blk == n_lat_blocks)

    @pl.when(s == 0)
    def _():
        h_scr[...] = jnp.zeros_like(h_scr)

    sp_lam = _softplus(-lam_ref[...])
    for b in range(nb):
        xp = jnp.where(seg_start, 0.0, xp_ref[b])
        xn = jnp.where(seg_end, 0.0, xn_ref[b])
        xc = _conv_block(ext_scr, x_ref[b], xp, xn, cw_ref, cb_ref, rows)
        ri = _dot(xc.astype(BF16), w_ref[...]) + bias_ref[...]
        r = _sigmoid(ri[:, :GROUP_W])
        gi = _sigmoid(ri[:, GROUP_W:])
        log_a = -LRU_C * r * sp_lam
        a_scr[b] = jnp.exp(log_a)
        u_scr[b] = jnp.sqrt(1.0 - jnp.exp(2.0 * log_a)) * (gi * xc)

    def step(g8, hs):
        hs = list(hs)
        for k in range(8):
            row = g8 * 8 + k
            if rev:
                row = rows - 1 - row
            for b in range(nb):
                h = a_scr[b, pl.ds(row, 1), :] * hs[b] + u_scr[b, pl.ds(row, 1), :]
                ho_scr[b, pl.ds(row, 1), :] = h
                hs[b] = h
        return tuple(hs)

    hs = lax.fori_loop(0, rows // 8, step, tuple(h_scr[b] for b in range(nb)))
    for b in range(nb):
        h_scr[b] = hs[b]
    if rev:
        for b in range(nb):
            gt = gate_ref[b]
            gelu = 0.5 * gt * (1.0 + jnp.tanh(math.sqrt(2.0 / math.pi) * (gt + 0.044715 * gt * gt * gt)))
            o_ref[b] = gelu * (hf_ref[b] + ho_scr[b])


def _lru_direction(rev, lru, hf, cw, cb, w_dir, bias_dir, lam_dir, t):
    nb = lru.shape[0]
    n_blocks = t // LRU_BLOCK
    n_lat_blocks = (t - CTX) // LRU_BLOCK

    def blk_of(s):
        if rev:
            return n_blocks - 1 - s
        return jnp.where(s == 0, n_lat_blocks, s - 1)

    def tok(col_blk):
        return pl.BlockSpec((nb, LRU_BLOCK, GROUP_W), lambda s: (0, blk_of(s), col_blk))

    def const(shape):
        return pl.BlockSpec(shape, lambda s: (0,) * len(shape))

    prev, nxt = _halo_specs(nb, LRU_BLOCK, GROUP_W, 1, blk_of, t)
    in_specs = [tok(1), prev, nxt]
    args = [lru, lru, lru]
    if rev:
        in_specs += [tok(0), tok(0)]
        args += [lru, hf]
    in_specs += [const((CONV_W, GROUP_W)), const((1, GROUP_W)), const((GROUP_W, 2 * GROUP_W)),
                 const((1, 2 * GROUP_W)), const((1, GROUP_W))]
    args += [cw, cb, w_dir, bias_dir, lam_dir]
    scratch = [pltpu.VMEM((nb, 1, GROUP_W), F32), pltpu.VMEM((nb, LRU_BLOCK, GROUP_W), F32),
               pltpu.VMEM((nb, LRU_BLOCK, GROUP_W), F32)]
    if rev:
        scratch.append(pltpu.VMEM((nb, LRU_BLOCK, GROUP_W), F32))
    scratch.append(pltpu.VMEM((LRU_BLOCK + 2 * HALO, GROUP_W), F32))
    return pl.pallas_call(
        functools.partial(_lru_kernel, rev, n_lat_blocks),
        grid=(n_blocks,),
        in_specs=in_specs,
        out_specs=pl.BlockSpec((nb, LRU_BLOCK, GROUP_W), lambda s: (0, blk_of(s), 0)),
        out_shape=jax.ShapeDtypeStruct((nb, t, GROUP_W), F32),
        scratch_shapes=scratch,
        compiler_params=_cparams(("arbitrary",)),
        name="lru_bwd" if rev else "lru_fwd",
    )(*args)


def _seg_mean(x2, seg_ref):
    return jnp.dot(x2, seg_ref[...], preferred_element_type=F32, precision=lax.Precision.HIGHEST)


def _rope(x, cos, sin_lo, sin_hi, half):
    w = x.shape[-1]
    rep = w // cos.shape[-1]
    cos, sin_lo, sin_hi = (jnp.concatenate([t] * rep, axis=1) if rep > 1 else t for t in (cos, sin_lo, sin_hi))
    return x * cos + pltpu.roll(x, w - half, 1) * sin_lo + pltpu.roll(x, half, 1) * sin_hi


def _prep_kernel(gqa_ref, diff_ref, gtab_ref, dtab_ref, seg64_ref, seg32_ref, gq_g_ref, gk_g_ref,
                 dq_g_ref, dk_g_ref, gq_ref, gk_ref, gv_ref, dq_ref, dk_ref, dv_ref):
    def norm_rope(x, seg_ref, g, tab_ref, half, scale):
        w = x.shape[-1]
        xn = x * lax.rsqrt(_seg_mean(x * x, seg_ref)[:, :w] + NORM_EPS) * g[:, :w]
        xr = _rope(xn, tab_ref[0], tab_ref[1], tab_ref[2], half)
        return xr * scale if scale != 1.0 else xr

    def scatter(ref, x, n, d):
        for hh in range(n):
            ref[0, hh] = x[:, hh * d:(hh + 1) * d].astype(ref.dtype)

    def scatter_t(ref, x, n, d):
        xt = x.T
        for hh in range(n):
            ref[0, hh, 0] = xt[hh * d:(hh + 1) * d, :].astype(ref.dtype)

    gqa = gqa_ref[...]
    kw = GQA_KV * GQA_D
    q = norm_rope(gqa[:, :GROUP_W], seg64_ref, gq_g_ref[...], gtab_ref, GQA_D // 4, GQA_D ** -0.5)
    k = _pad_lanes_rope(gqa[:, GROUP_W:GROUP_W + kw], seg64_ref, gk_g_ref[...], gtab_ref, GQA_D // 4)
    scatter(gq_ref, q, GQA_HEADS, GQA_D)
    scatter_t(gk_ref, k, GQA_KV, GQA_D)
    scatter(gv_ref, gqa[:, GROUP_W + kw:], GQA_KV, GQA_D)
    diff = diff_ref[...]
    dq = norm_rope(diff[:, :GROUP_W], seg32_ref, dq_g_ref[...], dtab_ref, DIFF_QK // 4, DIFF_QK ** -0.5)
    dk = norm_rope(diff[:, GROUP_W:2 * GROUP_W], seg32_ref, dk_g_ref[...], dtab_ref, DIFF_QK // 4, 1.0)
    scatter(dq_ref, dq, 2 * DIFF_HEADS, DIFF_QK)
    scatter_t(dk_ref, dk, 2 * DIFF_HEADS, DIFF_QK)
    scatter(dv_ref, diff[:, 2 * GROUP_W:], DIFF_HEADS, DIFF_V)


def _pad_lanes_rope(x, seg_ref, g, tab_ref, half):
    w = x.shape[-1]
    xn = x * lax.rsqrt(jnp.dot(x * x, seg_ref[:w, :w], preferred_element_type=F32,
                               precision=lax.Precision.HIGHEST) + NORM_EPS) * g[:, :w]
    return _rope(xn, tab_ref[0], tab_ref[1], tab_ref[2], half)


def _attn_prep(gqa, diff, gtab, dtab, seg64, seg32, gq_g, gk_g, dq_g, dk_g, nb, t):
    tpb = t // ROW_TILE

    def heads(n, d):
        return (pl.BlockSpec((1, n, ROW_TILE, d), lambda i: (i // tpb, 0, i % tpb, 0)),
                jax.ShapeDtypeStruct((nb, n, t, d), BF16))

    def heads_t(n, d):
        return (pl.BlockSpec((1, n, 1, d, ROW_TILE), lambda i: (i // tpb, 0, i % tpb, 0, 0)),
                jax.ShapeDtypeStruct((nb, n, tpb, d, ROW_TILE), BF16))

    outs = [heads(GQA_HEADS, GQA_D), heads_t(GQA_KV, GQA_D), heads(GQA_KV, GQA_D),
            heads(2 * DIFF_HEADS, DIFF_QK), heads_t(2 * DIFF_HEADS, DIFF_QK), heads(DIFF_HEADS, DIFF_V)]

    def const(shape):
        return pl.BlockSpec(shape, lambda i: (0,) * len(shape))

    return pl.pallas_call(
        _prep_kernel,
        grid=(nb * tpb,),
        in_specs=[pl.BlockSpec((ROW_TILE, GQA_COLS), lambda i: (i, 0)),
                  pl.BlockSpec((ROW_TILE, DIFF_COLS), lambda i: (i, 0)),
                  pl.BlockSpec((3, ROW_TILE, 128), lambda i: (0, i % tpb, 0)),
                  pl.BlockSpec((3, ROW_TILE, 128), lambda i: (0, i % tpb, 0)),
                  const((GROUP_W, GROUP_W)), const((GROUP_W, GROUP_W)),
                  const((1, GROUP_W)), const((1, GROUP_W)), const((1, GROUP_W)), const((1, GROUP_W))],
        out_specs=[o[0] for o in outs],
        out_shape=[o[1] for o in outs],
        compiler_params=_cparams(("arbitrary",)),
        name="attn_prep",
    )(gqa, diff, gtab, dtab, seg64, seg32, gq_g, gk_g, dq_g, dk_g)


def _attn_kernel(diff_mode, post_scale, unroll, q_ref, kt_ref, v_ref, *rest):
    if diff_mode:
        lam_ref, sg_ref, o_ref, s_scr, m_scr, l_scr, acc_scr = rest
    else:
        o_ref, s_scr, m_scr, l_scr, acc_scr = rest
    tq = q_ref.shape[2]
    n_tiles = kt_ref.shape[2]
    n_units = v_ref.shape[1]
    is_ctx = pl.program_id(2) == pl.num_programs(2) - 1

    def over_tiles(ctx_only, fn):
        if ctx_only:
            fn([n_tiles - 1])
        else:
            def body(i, carry):
                fn([i * unroll + j for j in range(unroll)])
                return carry
            lax.fori_loop(0, n_tiles // unroll, body, 0)

    def run_unit(u, ctx_only):
        if diff_mode:
            blocks = [(j * tq, q_ref[0, 2 * u + j], 2 * u + j) for j in range(2)]
        else:
            blocks = [(0, q_ref[0].reshape(2 * tq, q_ref.shape[-1]), u)]
        m_scr[...] = jnp.full_like(m_scr, -jnp.inf)

        def scores(kts):
            for kt in kts:
                for r0, q, kh in blocks:
                    r = q.shape[0]
                    s = _dot(q, kt_ref[0, kh, kt])
                    s_scr[kt, r0:r0 + r, :] = s
                    m_scr[r0:r0 + r, :] = jnp.maximum(m_scr[r0:r0 + r, :], jnp.maximum(s[:, :128], s[:, 128:]))

        over_tiles(ctx_only, scores)
        m_scr[...] = jnp.broadcast_to(jnp.max(m_scr[...], axis=-1, keepdims=True), m_scr.shape)
        l_scr[...] = jnp.zeros_like(l_scr)
        acc_scr[...] = jnp.zeros_like(acc_scr)

        def accumulate(kts):
            acc = None
            for kt in kts:
                m_rep = m_scr[...]
                p = jnp.exp(s_scr[kt] - jnp.concatenate([m_rep, m_rep], axis=1))
                l_scr[...] += p[:, :128] + p[:, 128:]
                off = kt * KEY_TILE if isinstance(kt, int) else pl.multiple_of(kt * KEY_TILE, KEY_TILE)
                d = _dot(p.astype(BF16), v_ref[0, u, pl.ds(off, KEY_TILE), :])
                acc = d if acc is None else acc + d
            acc_scr[...] += acc

        over_tiles(ctx_only, accumulate)
        return acc_scr[...] / jnp.sum(l_scr[...], axis=-1, keepdims=True)

    def finish(ctx_only):
        outs = []
        for u in range(n_units):
            o = run_unit(u, ctx_only)
            if diff_mode:
                o = o[:tq] - lam_ref[:, :o.shape[-1]] * o[tq:]
                outs.append(_rms(o, sg_ref[...]) * post_scale)
            else:
                outs += [o[:tq], o[tq:]]
        o_ref[0] = jnp.concatenate(outs, axis=1)

    @pl.when(is_ctx)
    def _():
        finish(True)

    @pl.when(jnp.logical_not(is_ctx))
    def _():
        finish(False)


def _attention(diff_mode, q, kt, v, lam_row, subln_g, post_scale):
    nb, _, t, dqk = q.shape
    dv = v.shape[-1]
    n_tiles = kt.shape[2]
    n_units = 2 if diff_mode else 1
    n_groups = v.shape[1] // n_units
    qh = q.shape[1] // n_groups
    kh = kt.shape[1] // n_groups
    assert kt.shape[-1] == KEY_TILE
    unroll = max(u for u in range(1, ATT_MAX_UNROLL + 1) if n_tiles % u == 0)
    in_specs = [pl.BlockSpec((1, qh, ATT_TQ, dqk), lambda b, g, i: (b, g, i, 0)),
                pl.BlockSpec((1, kh, n_tiles, dqk, KEY_TILE), lambda b, g, i: (b, g, 0, 0, 0)),
                pl.BlockSpec((1, n_units, t, dv), lambda b, g, i: (b, g, 0, 0))]
    args = [q, kt, v]
    if diff_mode:
        in_specs += [pl.BlockSpec((1, 128), lambda b, g, i: (0, 0)), pl.BlockSpec((1, dv), lambda b, g, i: (0, 0))]
        args += [lam_row, subln_g]
    return pl.pallas_call(
        functools.partial(_attn_kernel, diff_mode, post_scale, unroll),
        grid=(nb, n_groups, t // ATT_TQ),
        in_specs=in_specs,
        out_specs=pl.BlockSpec((1, ATT_TQ, 128), lambda b, g, i: (b, i, g)),
        out_shape=jax.ShapeDtypeStruct((nb, t, GROUP_W), F32),
        scratch_shapes=[pltpu.VMEM((n_tiles, 2 * ATT_TQ, KEY_TILE), F32), pltpu.VMEM((2 * ATT_TQ, 128), F32),
                        pltpu.VMEM((2 * ATT_TQ, 128), F32), pltpu.VMEM((2 * ATT_TQ, dv), F32)],
        compiler_params=_cparams(("arbitrary", "arbitrary", "arbitrary")),
        name="diff_attn" if diff_mode else "gqa_attn",
    )(*args)


def _outmlp_kernel(h_ref, ya_ref, yb_ref, yc_ref, yd_ref, mod_ref, g_ref, wo_ref, w1_ref, w2_ref, o_ref):
    mix = jnp.concatenate([ya_ref[...], yb_ref[...], yc_ref[...], yd_ref[...]], axis=1).astype(BF16)
    h1 = h_ref[...] + mod_ref[0, 2:3, :] * _dot(mix, wo_ref[...])
    v = _rms(h1, g_ref[...]) * (1.0 + mod_ref[0, 4:5, :]) + mod_ref[0, 3:4, :]
    u = jnp.maximum(_dot(v.astype(BF16), w1_ref[...]), 0.0)
    o_ref[...] = h1 + mod_ref[0, 5:6, :] * _dot((u * u).astype(BF16), w2_ref[...])


def _out_mlp(h, ya, yb, yc, yd, mod3, g2, w_out, w1, w2, tpb):
    n = h.shape[0]

    def tok(w):
        return pl.BlockSpec((ROW_TILE, w), lambda i: (i, 0))

    def const(shape):
        return pl.BlockSpec(shape, lambda i: (0, 0), pipeline_mode=pl.Buffered(1))

    return pl.pallas_call(
        _outmlp_kernel,
        grid=(n // ROW_TILE,),
        in_specs=[tok(D_MODEL), tok(GROUP_W), tok(GROUP_W), tok(GROUP_W), tok(GROUP_W),
                  pl.BlockSpec((1, N_MOD, D_MODEL), lambda i: (_mod_row(i, tpb), 0, 0)),
                  const((1, D_MODEL)), const((D_MODEL, D_MODEL)), const((D_MODEL, D_FF)), const((D_FF, D_MODEL))],
        out_specs=tok(D_MODEL),
        out_shape=jax.ShapeDtypeStruct((n, D_MODEL), F32),
        compiler_params=_cparams(("arbitrary",)),
        name="out_mlp",
    )(h, ya, yb, yc, yd, mod3, g2, w_out, w1, w2)


def _rope_tables(s_len, head_dim, lanes=128):
    m = head_dim // 2
    half = m // 2
    lane = np.arange(lanes)
    d = lane % head_dim
    freq = ROPE_THETA ** (-(d % half).astype(np.float64) / half)
    t = np.arange(s_len)
    pos = np.where((d < m)[None, :], (t // GRID_W)[:, None], (t % GRID_W)[:, None]).astype(np.float32)
    ang = pos * freq.astype(np.float32)[None, :]
    cos, sin = np.cos(ang), np.sin(ang)
    low = ((d % m) < half)[None, :]
    tab = np.stack([cos, np.where(low, -sin, 0.0), np.where(low, 0.0, sin)])
    ident = np.stack([np.ones((CTX, lanes)), np.zeros((CTX, lanes)), np.zeros((CTX, lanes))])
    return jnp.asarray(np.concatenate([tab, ident], axis=1), F32)


def _seg_matrix(width, seg):
    idx = np.arange(width) // seg
    return jnp.asarray((idx[:, None] == idx[None, :]).astype(np.float32) / seg)


def _lane_tile(v, width):
    return jnp.tile(v, width // v.shape[-1]).reshape(1, width)


def _block_diag(w):
    nblk, j, k = w.shape
    eye = jnp.eye(nblk, dtype=w.dtype)
    return jnp.einsum('njk,nm->njmk', w, eye).reshape(nblk * j, nblk * k)


def kernel(x, c, ctx, c_ctx, w_mod, b_mod, norm1_g, w_in, ssd_conv_w, ssd_conv_b, ssd_a_log, ssd_dt_bias, ssd_d, ssd_norm_g, gqa_q_norm_g, gqa_k_norm_g, lru_conv_w, lru_conv_b, lru_w_r, lru_b_r, lru_w_i, lru_b_i, lru_lambda, diff_q_norm_g, diff_k_norm_g, diff_lambda_q1, diff_lambda_k1, diff_lambda_q2, diff_lambda_k2, diff_subln_g, w_out, norm2_g, w_mlp1, w_mlp2):
    nb, s_len, d_model = x.shape
    depth = w_mod.shape[0]
    assert d_model == D_MODEL and ctx.shape[1] == CTX and s_len % ROW_TILE == 0 and s_len % GRID_W == 0
    t = s_len + CTX
    tpb = t // ROW_TILE
    n = nb * t

    ssd_cols = GROUP_W + SSD_XBC + 2 * SSD_HEADS
    w_in_p = jnp.concatenate(
        [w_in[:, :, :GROUP_W + SSD_XBC], w_in[:, :, ssd_cols:], w_in[:, :, GROUP_W + SSD_XBC:ssd_cols],
         jnp.zeros((depth, D_MODEL, DT_PAD - 2 * SSD_HEADS), w_in.dtype)], axis=-1).astype(BF16)
    w_out_b, w1_b, w2_b = w_out.astype(BF16), w_mlp1.astype(BF16), w_mlp2.astype(BF16)
    lru_w = jnp.stack([jnp.concatenate([_block_diag(lru_w_r[l, d]), _block_diag(lru_w_i[l, d])], axis=1)
                       for l in range(depth) for d in range(2)]).reshape(depth, 2, GROUP_W, 2 * GROUP_W).astype(BF16)
    lru_b = jnp.concatenate([lru_b_r, lru_b_i], axis=-1)
    pad8 = DT_PAD - 2 * SSD_HEADS
    dtb_rows = jnp.pad(ssd_dt_bias.reshape(depth, 1, 2 * SSD_HEADS), ((0, 0), (0, 0), (0, pad8)))
    a_rows = jnp.pad(-jnp.exp(ssd_a_log.reshape(depth, 1, 2 * SSD_HEADS)), ((0, 0), (0, 0), (0, pad8)))
    d_rows = jnp.repeat(ssd_d, SSD_P, axis=-1)
    lam_diff = (jnp.exp(jnp.sum(diff_lambda_q1 * diff_lambda_k1, axis=-1))
                - jnp.exp(jnp.sum(diff_lambda_q2 * diff_lambda_k2, axis=-1)))
    gtab = _rope_tables(s_len, GQA_D)
    dtab = _rope_tables(s_len, DIFF_QK)
    seg64 = _seg_matrix(GROUP_W, GQA_D)
    seg32 = _seg_matrix(GROUP_W, DIFF_QK)

    c8 = jnp.concatenate([c, c_ctx[None, :], jnp.zeros((8 - nb - 1, D_MODEL), F32)], axis=0)
    mod = _modulation(c8, w_mod, b_mod)
    h = jnp.concatenate([x, ctx], axis=1).reshape(n, D_MODEL)

    for l in range(depth):
        lam_init = 0.8 - 0.6 * math.exp(-0.3 * l)
        mod3 = mod[l].reshape(8, N_MOD, D_MODEL)
        z, xbc, gqa, lru, diff, dt = _in_proj(h, mod3, norm1_g[l].reshape(1, -1), w_in_p[l], tpb)
        r3 = lambda a: a.reshape(nb, t, a.shape[-1])
        ssd_args = (r3(xbc), r3(dt), r3(z))
        cw, cb = ssd_conv_w[l], ssd_conv_b[l].reshape(1, -1)
        ya_f = _ssd_direction(False, *ssd_args, None, cw, cb, dtb_rows[l], a_rows[l], d_rows[l, 0:1], None, t)
        ya = _ssd_direction(True, *ssd_args, ya_f, cw, cb, dtb_rows[l], a_rows[l], d_rows[l, 1:2],
                            ssd_norm_g[l].reshape(1, -1), t)
        lcw, lcb = lru_conv_w[l], lru_conv_b[l].reshape(1, -1)
        hf = _lru_direction(False, r3(lru), None, lcw, lcb, lru_w[l, 0], lru_b[l, 0:1], lru_lambda[l, 0:1], t)
        yc = _lru_direction(True, r3(lru), hf, lcw, lcb, lru_w[l, 1], lru_b[l, 1:2], lru_lambda[l, 1:2], t)
        gq, gk, gv, dq, dk, dv = _attn_prep(
            gqa, diff, gtab, dtab, seg64, seg32, _lane_tile(gqa_q_norm_g[l], GROUP_W),
            _lane_tile(gqa_k_norm_g[l], GROUP_W), _lane_tile(diff_q_norm_g[l], GROUP_W),
            _lane_tile(diff_k_norm_g[l], GROUP_W), nb, t)
        yb = _attention(False, gq, gk, gv, None, None, 1.0)
        lam_row = jnp.full((1, 128), lam_init, F32) + lam_diff[l]
        yd = _attention(True, dq, dk, dv, lam_row, diff_subln_g[l].reshape(1, -1), 1.0 - lam_init)
        h = _out_mlp(h, ya.reshape(n, -1), yb.reshape(n, -1), yc.reshape(n, -1), yd.reshape(n, -1), mod3,
                     norm2_g[l].reshape(1, -1), w_out_b[l], w1_b[l], w2_b[l], tpb)
    return h.reshape(nb, t, D_MODEL)[:, :s_len]
```

```python
import functools
import math

import jax
import jax.numpy as jnp
import numpy as np
from jax import lax
from jax.experimental import pallas as pl
from jax.experimental.pallas import tpu as pltpu

F32 = jnp.float32
BF16 = jnp.bfloat16

D_MODEL = 1024
CTX = 256
GROUP_W = 256
D_FF = 4 * D_MODEL
N_MOD = 6
NORM_EPS = 1e-6
ROPE_THETA = 10000.0
GRID_W = 64
CONV_W = 4
CONV_PAD_L = CONV_W // 2
HALO = 8

SSD_HEADS = 4
SSD_P = 64
SSD_GROUPS = 2
SSD_N = 128
SSD_CHUNK = 128
SSD_XBC = GROUP_W + 2 * SSD_GROUPS * SSD_N
GQA_HEADS = 4
GQA_KV = 2
GQA_D = 64
GQA_COLS = GROUP_W + 2 * GQA_KV * GQA_D
LRU_C = 8.0
LRU_COLS = 2 * GROUP_W
DIFF_HEADS = 4
DIFF_V = 64
DIFF_QK = 32
DIFF_COLS = 3 * GROUP_W
DT_PAD = 128
IN_COLS_P = GROUP_W + SSD_XBC + GQA_COLS + LRU_COLS + DIFF_COLS + DT_PAD

ROW_TILE = 256
LRU_BLOCK = 256
ATT_TQ = 256
KEY_TILE = 256
ATT_MAX_UNROLL = 33
V_PAD = 128
LOG2E = math.log2(math.e)
VMEM_LIMIT = 56 * 1024 * 1024


def _cparams(sem):
    return pltpu.CompilerParams(dimension_semantics=sem, vmem_limit_bytes=VMEM_LIMIT)


def _sigmoid(x):
    return 1.0 / (1.0 + jnp.exp(-x))


def _silu(x):
    return x * _sigmoid(x)


def _softplus(x):
    return jnp.maximum(x, 0.0) + jnp.log(1.0 + jnp.exp(-jnp.abs(x)))


def _rms(x, g):
    ms = jnp.mean(x * x, axis=-1, keepdims=True)
    return x * lax.rsqrt(ms + NORM_EPS) * g


def _dot(a, b):
    return jnp.dot(a, b, preferred_element_type=F32)


def _dot_nt(a, b):
    return lax.dot_general(a, b, (((1,), (1,)), ((), ())), preferred_element_type=F32)


def _dot_tn(a, b):
    return lax.dot_general(a, b, (((0,), (0,)), ((), ())), preferred_element_type=F32)


def _mod_kernel(c_ref, w_ref, b_ref, o_ref):
    act = _silu(c_ref[...]).astype(BF16)
    o_ref[0] = _dot(act, w_ref[0].astype(BF16)) + b_ref[0]


def _modulation(c8, w_mod, b_mod):
    depth = w_mod.shape[0]
    tn = 1536
    return pl.pallas_call(
        _mod_kernel,
        grid=(depth, (N_MOD * D_MODEL) // tn),
        in_specs=[pl.BlockSpec((8, D_MODEL), lambda l, j: (0, 0)),
                  pl.BlockSpec((1, D_MODEL, tn), lambda l, j: (l, 0, j)),
                  pl.BlockSpec((1, 1, tn), lambda l, j: (l, 0, j))],
        out_specs=pl.BlockSpec((1, 8, tn), lambda l, j: (l, 0, j)),
        out_shape=jax.ShapeDtypeStruct((depth, 8, N_MOD * D_MODEL), F32),
        compiler_params=_cparams(("arbitrary", "arbitrary")),
        name="modulation",
    )(c8, w_mod, b_mod.reshape(depth, 1, -1))


def _mod_row(i, tpb):
    b = i // tpb
    return jnp.where(i % tpb == tpb - 1, 2, b)


def _inproj_kernel(h_ref, mod_ref, g_ref, w_ref, z_ref, xbc_ref, gqa_ref, lru_ref, diff_ref, dt_ref):
    x = h_ref[...]
    u = _rms(x, g_ref[...]) * (1.0 + mod_ref[0, 1:2, :]) + mod_ref[0, 0:1, :]
    p = _dot(u.astype(BF16), w_ref[...])
    c0 = 0
    for ref in (z_ref, xbc_ref, gqa_ref, lru_ref, diff_ref, dt_ref):
        w = ref.shape[-1]
        ref[...] = p[:, c0:c0 + w]
        c0 += w


def _in_proj(h, mod3, g, w_in_p, tpb):
    n = h.shape[0]
    widths = (GROUP_W, SSD_XBC, GQA_COLS, LRU_COLS, DIFF_COLS, DT_PAD)
    return pl.pallas_call(
        _inproj_kernel,
        grid=(n // ROW_TILE,),
        in_specs=[pl.BlockSpec((ROW_TILE, D_MODEL), lambda i: (i, 0)),
                  pl.BlockSpec((1, N_MOD, D_MODEL), lambda i: (_mod_row(i, tpb), 0, 0)),
                  pl.BlockSpec((1, D_MODEL), lambda i: (0, 0)),
                  pl.BlockSpec((D_MODEL, IN_COLS_P), lambda i: (0, 0))],
        out_specs=[pl.BlockSpec((ROW_TILE, w), lambda i: (i, 0)) for w in widths],
        out_shape=[jax.ShapeDtypeStruct((n, w), F32) for w in widths],
        compiler_params=_cparams(("arbitrary",)),
        name="in_proj",
    )(h, mod3, g, w_in_p)


def _conv_block(ext_ref, x, xp, xn, w_ref, b_ref, rows):
    ext_ref[0:HALO, :] = xp
    ext_ref[HALO:HALO + rows, :] = x
    ext_ref[HALO + rows:2 * HALO + rows, :] = xn
    acc = b_ref[...] + w_ref[0:1, :] * ext_ref[pl.ds(HALO - CONV_PAD_L, rows), :]
    for j in range(1, CONV_W):
        acc = acc + w_ref[j:j + 1, :] * ext_ref[pl.ds(HALO - CONV_PAD_L + j, rows), :]
    return acc


def _halo_specs(nb, rows, width, col_blk, blk_of_step, t):
    per = rows // HALO
    last = t // HALO - 1
    prev = pl.BlockSpec((nb, HALO, width), lambda s: (0, jnp.maximum(blk_of_step(s) * per - 1, 0), col_blk))
    nxt = pl.BlockSpec((nb, HALO, width), lambda s: (0, jnp.minimum((blk_of_step(s) + 1) * per, last), col_blk))
    return prev, nxt


def _ssd_kernel(rev, n_lat_chunks, *refs):
    if rev:
        (xbc_ref, xp_ref, xn_ref, dt_ref, z_ref, yf_ref, cw_ref, cb_ref, dtb_ref, a_ref, d_ref, ng_ref,
         o_ref, h_scr, ext_scr) = refs
    else:
        (xbc_ref, xp_ref, xn_ref, dt_ref, cw_ref, cb_ref, dtb_ref, a_ref, d_ref,
         o_ref, h_scr, ext_scr) = refs
    nb = xbc_ref.shape[0]
    q = SSD_CHUNK
    s = pl.program_id(0)
    n_steps = pl.num_programs(0)
    if rev:
        chunk = n_steps - 1 - s
    else:
        chunk = jnp.where(s < 2, n_lat_chunks + s, s - 2)
    seg_start = (chunk == 0) | (chunk == n_lat_chunks)
    seg_end = (chunk == n_lat_chunks - 1) | (chunk == n_lat_chunks + 1)

    @pl.when(s == 0)
    def _():
        h_scr[...] = jnp.zeros_like(h_scr)

    li = lax.broadcasted_iota(jnp.int32, (q, q), 0)
    si = lax.broadcasted_iota(jnp.int32, (q, q), 1)
    keep = (li <= si) if rev else (li >= si)
    tri = keep.astype(F32)
    last = 0 if rev else q - 1
    col0 = SSD_HEADS if rev else 0

    for b in range(nb):
        xp = jnp.where(seg_start, 0.0, xp_ref[b])
        xn = jnp.where(seg_end, 0.0, xn_ref[b])
        xbc = _silu(_conv_block(ext_scr, xbc_ref[b], xp, xn, cw_ref, cb_ref, q))
        xs = xbc[:, 0:GROUP_W]
        sp = _softplus(dt_ref[b] + dtb_ref[...])
        da = sp * a_ref[...]
        acum = jnp.dot(tri, da, preferred_element_type=F32, precision=lax.Precision.HIGHEST)
        acum_t = acum.T
        ys = []
        for g in range(SSD_GROUPS):
            bm = xbc[:, GROUP_W + g * SSD_N:GROUP_W + (g + 1) * SSD_N]
            cm = xbc[:, GROUP_W + (SSD_GROUPS + g) * SSD_N:GROUP_W + (SSD_GROUPS + g + 1) * SSD_N]
            cb = _dot_nt(cm.astype(BF16), bm.astype(BF16))
            for hh in range(g * (SSD_HEADS // SSD_GROUPS), (g + 1) * (SSD_HEADS // SSD_GROUPS)):
                c = col0 + hh
                ac = acum[:, c:c + 1]
                ar = acum_t[c:c + 1, :]
                tot = acum[last:last + 1, c:c + 1]
                decay = jnp.exp(jnp.where(keep, ac - ar, -jnp.inf))
                xh = xs[:, hh * SSD_P:(hh + 1) * SSD_P]
                xdt = (xh * sp[:, c:c + 1]).astype(BF16)
                y = _dot((cb * decay).astype(BF16), xdt)
                hin = h_scr[b, hh]
                y = y + _dot((cm * jnp.exp(ac)).astype(BF16), hin.astype(BF16))
                bw = (bm * jnp.exp(tot - ac)).astype(BF16)
                h_scr[b, hh] = hin * jnp.exp(tot) + _dot_tn(bw, xdt)
                ys.append(y)
        y = jnp.concatenate(ys, axis=1) + d_ref[...] * xs
        if rev:
            y = (y + yf_ref[b]) * _silu(z_ref[b])
            half = GROUP_W // SSD_GROUPS
            y = jnp.concatenate([_rms(y[:, :half], ng_ref[:, :half]), _rms(y[:, half:], ng_ref[:, half:])], axis=1)
        o_ref[b] = y


def _ssd_direction(rev, xbc, dt, z, yf, cw, cb, dtb, a_row, d_row, ng, t):
    nb = xbc.shape[0]
    n_chunks = t // SSD_CHUNK
    n_lat_chunks = (t - CTX) // SSD_CHUNK

    def chunk_of(s):
        if rev:
            return n_chunks - 1 - s
        return jnp.where(s < 2, n_lat_chunks + s, s - 2)

    def tok(width):
        return pl.BlockSpec((nb, SSD_CHUNK, width), lambda s: (0, chunk_of(s), 0))

    def const(shape):
        return pl.BlockSpec(shape, lambda s: (0,) * len(shape))

    prev, nxt = _halo_specs(nb, SSD_CHUNK, SSD_XBC, 0, chunk_of, t)
    in_specs = [tok(SSD_XBC), prev, nxt, tok(DT_PAD)]
    args = [xbc, xbc, xbc, dt]
    if rev:
        in_specs += [tok(GROUP_W), tok(GROUP_W)]
        args += [z, yf]
    in_specs += [const((CONV_W, SSD_XBC)), const((1, SSD_XBC)), const((1, DT_PAD)), const((1, DT_PAD)),
                 const((1, GROUP_W))]
    args += [cw, cb, dtb, a_row, d_row]
    if rev:
        in_specs.append(const((1, GROUP_W)))
        args.append(ng)
    return pl.pallas_call(
        functools.partial(_ssd_kernel, rev, n_lat_chunks),
        grid=(n_chunks,),
        in_specs=in_specs,
        out_specs=tok(GROUP_W),
        out_shape=jax.ShapeDtypeStruct((nb, t, GROUP_W), F32),
        scratch_shapes=[pltpu.VMEM((nb, SSD_HEADS, SSD_N, SSD_P), F32),
                        pltpu.VMEM((SSD_CHUNK + 2 * HALO, SSD_XBC), F32)],
        compiler_params=_cparams(("arbitrary",)),
        name="ssd_bwd" if rev else "ssd_fwd",
    )(*args)


def _lru_kernel(rev, n_lat_blocks, *refs):
    if rev:
        (x_ref, xp_ref, xn_ref, gate_ref, hf_ref, cw_ref, cb_ref, w_ref, bias_ref, lam_ref,
         o_ref, h_scr, a_scr, u_scr, ho_scr, ext_scr) = refs
    else:
        (x_ref, xp_ref, xn_ref, cw_ref, cb_ref, w_ref, bias_ref, lam_ref,
         o_ref, h_scr, a_scr, u_scr, ext_scr) = refs
        ho_scr = o_ref
    nb = x_ref.shape[0]
    rows = LRU_BLOCK
    s = pl.program_id(0)
    n_steps = pl.num_programs(0)
    if rev:
        blk = n_steps - 1 - s
    else:
        blk = jnp.where(s == 0, n_lat_blocks, s - 1)
    seg_start = (blk == 0) | (blk == n_lat_blocks)
    seg_end = (blk == n_lat_blocks - 1) | (blk == n_lat_blocks)

    @pl.when(s == 0)
    def _():
        h_scr[...] = jnp.zeros_like(h_scr)

    sp_lam = _softplus(-lam_ref[...])
    for b in range(nb):
        xp = jnp.where(seg_start, 0.0, xp_ref[b])
        xn = jnp.where(seg_end, 0.0, xn_ref[b])
        xc = _conv_block(ext_scr, x_ref[b], xp, xn, cw_ref, cb_ref, rows)
        ri = _dot(xc.astype(BF16), w_ref[...]) + bias_ref[...]
        r = _sigmoid(ri[:, :GROUP_W])
        gi = _sigmoid(ri[:, GROUP_W:])
        log_a = -LRU_C * r * sp_lam
        a_scr[b] = jnp.exp(log_a)
        u_scr[b] = jnp.sqrt(1.0 - jnp.exp(2.0 * log_a)) * (gi * xc)

    def step(g8, hs):
        hs = list(hs)
        for k in range(8):
            row = g8 * 8 + k
            if rev:
                row = rows - 1 - row
            for b in range(nb):
                h = a_scr[b, pl.ds(row, 1), :] * hs[b] + u_scr[b, pl.ds(row, 1), :]
                ho_scr[b, pl.ds(row, 1), :] = h
                hs[b] = h
        return tuple(hs)

    hs = lax.fori_loop(0, rows // 8, step, tuple(h_scr[b] for b in range(nb)))
    for b in range(nb):
        h_scr[b] = hs[b]
    if rev:
        for b in range(nb):
            gt = gate_ref[b]
            gelu = 0.5 * gt * (1.0 + jnp.tanh(math.sqrt(2.0 / math.pi) * (gt + 0.044715 * gt * gt * gt)))
            o_ref[b] = gelu * (hf_ref[b] + ho_scr[b])


def _lru_direction(rev, lru, hf, cw, cb, w_dir, bias_dir, lam_dir, t):
    nb = lru.shape[0]
    n_blocks = t // LRU_BLOCK
    n_lat_blocks = (t - CTX) // LRU_BLOCK

    def blk_of(s):
        if rev:
            return n_blocks - 1 - s
        return jnp.where(s == 0, n_lat_blocks, s - 1)

    def tok(col_blk):
        return pl.BlockSpec((nb, LRU_BLOCK, GROUP_W), lambda s: (0, blk_of(s), col_blk))

    def const(shape):
        return pl.BlockSpec(shape, lambda s: (0,) * len(shape))

    prev, nxt = _halo_specs(nb, LRU_BLOCK, GROUP_W, 1, blk_of, t)
    in_specs = [tok(1), prev, nxt]
    args = [lru, lru, lru]
    if rev:
        in_specs += [tok(0), tok(0)]
        args += [lru, hf]
    in_specs += [const((CONV_W, GROUP_W)), const((1, GROUP_W)), const((GROUP_W, 2 * GROUP_W)),
                 const((1, 2 * GROUP_W)), const((1, GROUP_W))]
    args += [cw, cb, w_dir, bias_dir, lam_dir]
    scratch = [pltpu.VMEM((nb, 1, GROUP_W), F32), pltpu.VMEM((nb, LRU_BLOCK, GROUP_W), F32),
               pltpu.VMEM((nb, LRU_BLOCK, GROUP_W), F32)]
    if rev:
        scratch.append(pltpu.VMEM((nb, LRU_BLOCK, GROUP_W), F32))
    scratch.append(pltpu.VMEM((LRU_BLOCK + 2 * HALO, GROUP_W), F32))
    return pl.pallas_call(
        functools.partial(_lru_kernel, rev, n_lat_blocks),
        grid=(n_blocks,),
        in_specs=in_specs,
        out_specs=pl.BlockSpec((nb, LRU_BLOCK, GROUP_W), lambda s: (0, blk_of(s), 0)),
        out_shape=jax.ShapeDtypeStruct((nb, t, GROUP_W), F32),
        scratch_shapes=scratch,
        compiler_params=_cparams(("arbitrary",)),
        name="lru_bwd" if rev else "lru_fwd",
    )(*args)


def _seg_mean(x2, seg_ref):
    return jnp.dot(x2, seg_ref[...], preferred_element_type=F32, precision=lax.Precision.HIGHEST)


def _rope(x, cos, sin_lo, sin_hi, half):
    w = x.shape[-1]
    rep = w // cos.shape[-1]
    cos, sin_lo, sin_hi = (jnp.concatenate([t] * rep, axis=1) if rep > 1 else t for t in (cos, sin_lo, sin_hi))
    return x * cos + pltpu.roll(x, w - half, 1) * sin_lo + pltpu.roll(x, half, 1) * sin_hi


def _prep_kernel(gqa_ref, diff_ref, gtab_ref, dtab_ref, seg64_ref, seg32_ref, gq_g_ref, gk_g_ref,
                 dq_g_ref, dk_g_ref, gq_ref, gk_ref, gv_ref, dq_ref, dk_ref, dv_ref):
    def norm_rope(x, seg_ref, g, tab_ref, half, scale):
        w = x.shape[-1]
        xn = x * lax.rsqrt(_seg_mean(x * x, seg_ref)[:, :w] + NORM_EPS) * g[:, :w]
        xr = _rope(xn, tab_ref[0], tab_ref[1], tab_ref[2], half)
        return xr * scale if scale != 1.0 else xr

    def scatter(ref, x, n, d):
        for hh in range(n):
            ref[0, hh] = x[:, hh * d:(hh + 1) * d].astype(ref.dtype)

    def scatter_v(ref, x, n, d):
        ones = jnp.ones((x.shape[0], V_PAD - d), F32)
        for hh in range(n):
            ref[0, hh] = jnp.concatenate([x[:, hh * d:(hh + 1) * d], ones], axis=1).astype(ref.dtype)

    def scatter_t(ref, x, n, d):
        xt = x.T
        for hh in range(n):
            ref[0, hh, 0] = xt[hh * d:(hh + 1) * d, :].astype(ref.dtype)

    gqa = gqa_ref[...]
    kw = GQA_KV * GQA_D
    q = norm_rope(gqa[:, :GROUP_W], seg64_ref, gq_g_ref[...], gtab_ref, GQA_D // 4, GQA_D ** -0.5 * LOG2E)
    k = _pad_lanes_rope(gqa[:, GROUP_W:GROUP_W + kw], seg64_ref, gk_g_ref[...], gtab_ref, GQA_D // 4)
    scatter(gq_ref, q, GQA_HEADS, GQA_D)
    scatter_t(gk_ref, k, GQA_KV, GQA_D)
    scatter_v(gv_ref, gqa[:, GROUP_W + kw:], GQA_KV, GQA_D)
    diff = diff_ref[...]
    dq = norm_rope(diff[:, :GROUP_W], seg32_ref, dq_g_ref[...], dtab_ref, DIFF_QK // 4, DIFF_QK ** -0.5 * LOG2E)
    dk = norm_rope(diff[:, GROUP_W:2 * GROUP_W], seg32_ref, dk_g_ref[...], dtab_ref, DIFF_QK // 4, 1.0)
    scatter(dq_ref, dq, 2 * DIFF_HEADS, DIFF_QK)
    scatter_t(dk_ref, dk, 2 * DIFF_HEADS, DIFF_QK)
    scatter_v(dv_ref, diff[:, 2 * GROUP_W:], DIFF_HEADS, DIFF_V)


def _pad_lanes_rope(x, seg_ref, g, tab_ref, half):
    w = x.shape[-1]
    xn = x * lax.rsqrt(jnp.dot(x * x, seg_ref[:w, :w], preferred_element_type=F32,
                               precision=lax.Precision.HIGHEST) + NORM_EPS) * g[:, :w]
    return _rope(xn, tab_ref[0], tab_ref[1], tab_ref[2], half)


def _attn_prep(gqa, diff, gtab, dtab, seg64, seg32, gq_g, gk_g, dq_g, dk_g, nb, t):
    tpb = t // ROW_TILE

    def heads(n, d):
        return (pl.BlockSpec((1, n, ROW_TILE, d), lambda i: (i // tpb, 0, i % tpb, 0)),
                jax.ShapeDtypeStruct((nb, n, t, d), BF16))

    def heads_t(n, d):
        return (pl.BlockSpec((1, n, 1, d, ROW_TILE), lambda i: (i // tpb, 0, i % tpb, 0, 0)),
                jax.ShapeDtypeStruct((nb, n, tpb, d, ROW_TILE), BF16))

    outs = [heads(GQA_HEADS, GQA_D), heads_t(GQA_KV, GQA_D), heads(GQA_KV, V_PAD),
            heads(2 * DIFF_HEADS, DIFF_QK), heads_t(2 * DIFF_HEADS, DIFF_QK), heads(DIFF_HEADS, V_PAD)]

    def const(shape):
        return pl.BlockSpec(shape, lambda i: (0,) * len(shape))

    return pl.pallas_call(
        _prep_kernel,
        grid=(nb * tpb,),
        in_specs=[pl.BlockSpec((ROW_TILE, GQA_COLS), lambda i: (i, 0)),
                  pl.BlockSpec((ROW_TILE, DIFF_COLS), lambda i: (i, 0)),
                  pl.BlockSpec((3, ROW_TILE, 128), lambda i: (0, i % tpb, 0)),
                  pl.BlockSpec((3, ROW_TILE, 128), lambda i: (0, i % tpb, 0)),
                  const((GROUP_W, GROUP_W)), const((GROUP_W, GROUP_W)),
                  const((1, GROUP_W)), const((1, GROUP_W)), const((1, GROUP_W)), const((1, GROUP_W))],
        out_specs=[o[0] for o in outs],
        out_shape=[o[1] for o in outs],
        compiler_params=_cparams(("arbitrary",)),
        name="attn_prep",
    )(gqa, diff, gtab, dtab, seg64, seg32, gq_g, gk_g, dq_g, dk_g)


def _attn_kernel(diff_mode, post_scale, unroll, q_ref, kt_ref, v_ref, *rest):
    if diff_mode:
        lam_ref, sg_ref, o_ref, s_scr, m_scr, acc_scr = rest
    else:
        o_ref, s_scr, m_scr, acc_scr = rest
    tq = q_ref.shape[2]
    n_tiles = kt_ref.shape[2]
    n_units = v_ref.shape[1]
    dv = o_ref.shape[-1] // 2

    def over_tiles(ctx_only, fn):
        if ctx_only:
            fn([n_tiles - 1])
        else:
            def body(i, carry):
                fn([i * unroll + j for j in range(unroll)])
                return carry
            lax.fori_loop(0, n_tiles // unroll, body, 0)

    def run_unit(u, ctx_only):
        if diff_mode:
            blocks = [(j * tq, q_ref[0, 2 * u + j], 2 * u + j) for j in range(2)]
        else:
            blocks = [(0, q_ref[0].reshape(2 * tq, q_ref.shape[-1]), u)]
        m_scr[...] = jnp.full_like(m_scr, -jnp.inf)

        def scores(kts):
            for kt in kts:
                for r0, q, kh in blocks:
                    r = q.shape[0]
                    s = _dot(q, kt_ref[0, kh, kt])
                    s_scr[kt, r0:r0 + r, :] = s
                    m_scr[r0:r0 + r, :] = jnp.maximum(m_scr[r0:r0 + r, :], jnp.maximum(s[:, :128], s[:, 128:]))

        over_tiles(ctx_only, scores)
        m_scr[...] = jnp.broadcast_to(jnp.max(m_scr[...], axis=-1, keepdims=True), m_scr.shape)
        acc_scr[...] = jnp.zeros_like(acc_scr)

        def accumulate(kts):
            acc = None
            for kt in kts:
                m_rep = m_scr[...]
                p = jnp.exp2(s_scr[kt] - jnp.concatenate([m_rep, m_rep], axis=1))
                off = kt * KEY_TILE if isinstance(kt, int) else pl.multiple_of(kt * KEY_TILE, KEY_TILE)
                d = _dot(p.astype(BF16), v_ref[0, u, pl.ds(off, KEY_TILE), :])
                acc = d if acc is None else acc + d
            acc_scr[...] += acc

        over_tiles(ctx_only, accumulate)
        acc = acc_scr[...]
        return acc[:, :dv] / acc[:, dv:dv + 1]

    def finish(ctx_only):
        outs = []
        for u in range(n_units):
            o = run_unit(u, ctx_only)
            if diff_mode:
                o = o[:tq] - lam_ref[:, :o.shape[-1]] * o[tq:]
                outs.append(_rms(o, sg_ref[...]) * post_scale)
            else:
                outs += [o[:tq], o[tq:]]
        o_ref[0] = jnp.concatenate(outs, axis=1)

    is_ctx = pl.program_id(2) == pl.num_programs(2) - 1

    @pl.when(is_ctx)
    def _():
        finish(True)

    @pl.when(jnp.logical_not(is_ctx))
    def _():
        finish(False)


def _attention(diff_mode, q, kt, v, lam_row, subln_g, post_scale):
    nb, _, t, dqk = q.shape
    dv = GROUP_W // (GQA_HEADS if not diff_mode else DIFF_HEADS)
    n_tiles = kt.shape[2]
    n_units = 2 if diff_mode else 1
    n_groups = v.shape[1] // n_units
    qh = q.shape[1] // n_groups
    kh = kt.shape[1] // n_groups
    assert kt.shape[-1] == KEY_TILE
    unroll = max(u for u in range(1, ATT_MAX_UNROLL + 1) if n_tiles % u == 0)
    in_specs = [pl.BlockSpec((1, qh, ATT_TQ, dqk), lambda b, g, i: (b, g, i, 0)),
                pl.BlockSpec((1, kh, n_tiles, dqk, KEY_TILE), lambda b, g, i: (b, g, 0, 0, 0)),
                pl.BlockSpec((1, n_units, t, V_PAD), lambda b, g, i: (b, g, 0, 0))]
    args = [q, kt, v]
    if diff_mode:
        in_specs += [pl.BlockSpec((1, 128), lambda b, g, i: (0, 0)), pl.BlockSpec((1, dv), lambda b, g, i: (0, 0))]
        args += [lam_row, subln_g]
    return pl.pallas_call(
        functools.partial(_attn_kernel, diff_mode, post_scale, unroll),
        grid=(nb, n_groups, t // ATT_TQ),
        in_specs=in_specs,
        out_specs=pl.BlockSpec((1, ATT_TQ, 128), lambda b, g, i: (b, i, g)),
        out_shape=jax.ShapeDtypeStruct((nb, t, GROUP_W), F32),
        scratch_shapes=[pltpu.VMEM((n_tiles, 2 * ATT_TQ, KEY_TILE), F32), pltpu.VMEM((2 * ATT_TQ, 128), F32),
                        pltpu.VMEM((2 * ATT_TQ, V_PAD), F32)],
        compiler_params=_cparams(("arbitrary", "arbitrary", "arbitrary")),
        name="diff_attn" if diff_mode else "gqa_attn",
    )(*args)


def _outmlp_kernel(h_ref, ya_ref, yb_ref, yc_ref, yd_ref, mod_ref, g_ref, wo_ref, w1_ref, w2_ref, o_ref):
    mix = jnp.concatenate([ya_ref[...], yb_ref[...], yc_ref[...], yd_ref[...]], axis=1).astype(BF16)
    h1 = h_ref[...] + mod_ref[0, 2:3, :] * _dot(mix, wo_ref[...])
    v = _rms(h1, g_ref[...]) * (1.0 + mod_ref[0, 4:5, :]) + mod_ref[0, 3:4, :]
    u = jnp.maximum(_dot(v.astype(BF16), w1_ref[...]), 0.0)
    o_ref[...] = h1 + mod_ref[0, 5:6, :] * _dot((u * u).astype(BF16), w2_ref[...])


def _out_mlp(h, ya, yb, yc, yd, mod3, g2, w_out, w1, w2, tpb, latent_only):
    n = h.shape[0]
    lat = tpb - 1
    n_tiles = (n // ROW_TILE) // tpb * lat if latent_only else n // ROW_TILE

    def src(i):
        return (i // lat) * tpb + i % lat if latent_only else i

    def mod_row(i):
        return i // lat if latent_only else _mod_row(i, tpb)

    def tok(w):
        return pl.BlockSpec((ROW_TILE, w), lambda i: (src(i), 0))

    def const(shape):
        return pl.BlockSpec(shape, lambda i: (0, 0), pipeline_mode=pl.Buffered(1))

    return pl.pallas_call(
        _outmlp_kernel,
        grid=(n_tiles,),
        in_specs=[tok(D_MODEL), tok(GROUP_W), tok(GROUP_W), tok(GROUP_W), tok(GROUP_W),
                  pl.BlockSpec((1, N_MOD, D_MODEL), lambda i: (mod_row(i), 0, 0)),
                  const((1, D_MODEL)), const((D_MODEL, D_MODEL)), const((D_MODEL, D_FF)), const((D_FF, D_MODEL))],
        out_specs=pl.BlockSpec((ROW_TILE, D_MODEL), lambda i: (i, 0)),
        out_shape=jax.ShapeDtypeStruct((n_tiles * ROW_TILE, D_MODEL), F32),
        compiler_params=_cparams(("arbitrary",)),
        name="out_mlp",
    )(h, ya, yb, yc, yd, mod3, g2, w_out, w1, w2)


def _rope_tables(s_len, head_dim, lanes=128):
    m = head_dim // 2
    half = m // 2
    lane = np.arange(lanes)
    d = lane % head_dim
    freq = ROPE_THETA ** (-(d % half).astype(np.float64) / half)
    t = np.arange(s_len)
    pos = np.where((d < m)[None, :], (t // GRID_W)[:, None], (t % GRID_W)[:, None]).astype(np.float32)
    ang = pos * freq.astype(np.float32)[None, :]
    cos, sin = np.cos(ang), np.sin(ang)
    low = ((d % m) < half)[None, :]
    tab = np.stack([cos, np.where(low, -sin, 0.0), np.where(low, 0.0, sin)])
    ident = np.stack([np.ones((CTX, lanes)), np.zeros((CTX, lanes)), np.zeros((CTX, lanes))])
    return jnp.asarray(np.concatenate([tab, ident], axis=1), F32)


def _seg_matrix(width, seg):
    idx = np.arange(width) // seg
    return jnp.asarray((idx[:, None] == idx[None, :]).astype(np.float32) / seg)


def _lane_tile(v, width):
    return jnp.tile(v, width // v.shape[-1]).reshape(1, width)


def _block_diag(w):
    nblk, j, k = w.shape
    eye = jnp.eye(nblk, dtype=w.dtype)
    return jnp.einsum('njk,nm->njmk', w, eye).reshape(nblk * j, nblk * k)


def kernel(x, c, ctx, c_ctx, w_mod, b_mod, norm1_g, w_in, ssd_conv_w, ssd_conv_b, ssd_a_log, ssd_dt_bias, ssd_d, ssd_norm_g, gqa_q_norm_g, gqa_k_norm_g, lru_conv_w, lru_conv_b, lru_w_r, lru_b_r, lru_w_i, lru_b_i, lru_lambda, diff_q_norm_g, diff_k_norm_g, diff_lambda_q1, diff_lambda_k1, diff_lambda_q2, diff_lambda_k2, diff_subln_g, w_out, norm2_g, w_mlp1, w_mlp2):
    nb, s_len, d_model = x.shape
    depth = w_mod.shape[0]
    assert d_model == D_MODEL and ctx.shape[1] == CTX and s_len % ROW_TILE == 0 and s_len % GRID_W == 0
    t = s_len + CTX
    tpb = t // ROW_TILE
    n = nb * t

    ssd_cols = GROUP_W + SSD_XBC + 2 * SSD_HEADS
    w_in_p = jnp.concatenate(
        [w_in[:, :, :GROUP_W + SSD_XBC], w_in[:, :, ssd_cols:], w_in[:, :, GROUP_W + SSD_XBC:ssd_cols],
         jnp.zeros((depth, D_MODEL, DT_PAD - 2 * SSD_HEADS), w_in.dtype)], axis=-1).astype(BF16)
    w_out_b, w1_b, w2_b = w_out.astype(BF16), w_mlp1.astype(BF16), w_mlp2.astype(BF16)
    lru_w = jnp.stack([jnp.concatenate([_block_diag(lru_w_r[l, d]), _block_diag(lru_w_i[l, d])], axis=1)
                       for l in range(depth) for d in range(2)]).reshape(depth, 2, GROUP_W, 2 * GROUP_W).astype(BF16)
    lru_b = jnp.concatenate([lru_b_r, lru_b_i], axis=-1)
    pad8 = DT_PAD - 2 * SSD_HEADS
    dtb_rows = jnp.pad(ssd_dt_bias.reshape(depth, 1, 2 * SSD_HEADS), ((0, 0), (0, 0), (0, pad8)))
    a_rows = jnp.pad(-jnp.exp(ssd_a_log.reshape(depth, 1, 2 * SSD_HEADS)), ((0, 0), (0, 0), (0, pad8)))
    d_rows = jnp.repeat(ssd_d, SSD_P, axis=-1)
    lam_diff = (jnp.exp(jnp.sum(diff_lambda_q1 * diff_lambda_k1, axis=-1))
                - jnp.exp(jnp.sum(diff_lambda_q2 * diff_lambda_k2, axis=-1)))
    gtab = _rope_tables(s_len, GQA_D)
    dtab = _rope_tables(s_len, DIFF_QK)
    seg64 = _seg_matrix(GROUP_W, GQA_D)
    seg32 = _seg_matrix(GROUP_W, DIFF_QK)

    c8 = jnp.concatenate([c, c_ctx[None, :], jnp.zeros((8 - nb - 1, D_MODEL), F32)], axis=0)
    mod = _modulation(c8, w_mod, b_mod)
    h = jnp.concatenate([x, ctx], axis=1).reshape(n, D_MODEL)

    for l in range(depth):
        lam_init = 0.8 - 0.6 * math.exp(-0.3 * l)
        mod3 = mod[l].reshape(8, N_MOD, D_MODEL)
        z, xbc, gqa, lru, diff, dt = _in_proj(h, mod3, norm1_g[l].reshape(1, -1), w_in_p[l], tpb)
        r3 = lambda a: a.reshape(nb, t, a.shape[-1])
        ssd_args = (r3(xbc), r3(dt), r3(z))
        cw, cb = ssd_conv_w[l], ssd_conv_b[l].reshape(1, -1)
        ya_f = _ssd_direction(False, *ssd_args, None, cw, cb, dtb_rows[l], a_rows[l], d_rows[l, 0:1], None, t)
        ya = _ssd_direction(True, *ssd_args, ya_f, cw, cb, dtb_rows[l], a_rows[l], d_rows[l, 1:2],
                            ssd_norm_g[l].reshape(1, -1), t)
        lcw, lcb = lru_conv_w[l], lru_conv_b[l].reshape(1, -1)
        hf = _lru_direction(False, r3(lru), None, lcw, lcb, lru_w[l, 0], lru_b[l, 0:1], lru_lambda[l, 0:1], t)
        yc = _lru_direction(True, r3(lru), hf, lcw, lcb, lru_w[l, 1], lru_b[l, 1:2], lru_lambda[l, 1:2], t)
        gq, gk, gv, dq, dk, dv = _attn_prep(
            gqa, diff, gtab, dtab, seg64, seg32, _lane_tile(gqa_q_norm_g[l], GROUP_W),
            _lane_tile(gqa_k_norm_g[l], GROUP_W), _lane_tile(diff_q_norm_g[l], GROUP_W),
            _lane_tile(diff_k_norm_g[l], GROUP_W), nb, t)
        last = l == depth - 1
        yb = _attention(False, gq, gk, gv, None, None, 1.0)
        lam_row = jnp.full((1, 128), lam_init, F32) + lam_diff[l]
        yd = _attention(True, dq, dk, dv, lam_row, diff_subln_g[l].reshape(1, -1), 1.0 - lam_init)
        h = _out_mlp(h, ya.reshape(n, -1), yb.reshape(n, -1), yc.reshape(n, -1), yd.reshape(n, -1), mod3,
                     norm2_g[l].reshape(1, -1), w_out_b[l], w1_b[l], w2_b[l], tpb, last)
    return h.reshape(nb, s_len, D_MODEL)
```

```python
import functools
import math

import jax
import jax.numpy as jnp
import numpy as np
from jax import lax
from jax.experimental import pallas as pl
from jax.experimental.pallas import tpu as pltpu

F32 = jnp.float32
BF16 = jnp.bfloat16

D_MODEL = 1024
CTX = 256
GROUP_W = 256
D_FF = 4 * D_MODEL
N_MOD = 6
NORM_EPS = 1e-6
ROPE_THETA = 10000.0
GRID_W = 64
CONV_W = 4
CONV_PAD_L = CONV_W // 2
HALO = 8

SSD_HEADS = 4
SSD_P = 64
SSD_GROUPS = 2
SSD_N = 128
SSD_CHUNK = 128
SSD_XBC = GROUP_W + 2 * SSD_GROUPS * SSD_N
GQA_HEADS = 4
GQA_KV = 2
GQA_D = 64
GQA_COLS = GROUP_W + 2 * GQA_KV * GQA_D
LRU_C = 8.0
LRU_COLS = 2 * GROUP_W
DIFF_HEADS = 4
DIFF_V = 64
DIFF_QK = 32
DIFF_COLS = 3 * GROUP_W
DT_PAD = 128
IN_COLS_P = GROUP_W + SSD_XBC + GQA_COLS + LRU_COLS + DIFF_COLS + DT_PAD

ROW_TILE = 256
LRU_BLOCK = 256
ATT_TQ = 256
KEY_TILE = 256
ATT_MAX_UNROLL = 33
V_PAD = 128
LOG2E = math.log2(math.e)
VMEM_LIMIT = 56 * 1024 * 1024


def _cparams(sem):
    return pltpu.CompilerParams(dimension_semantics=sem, vmem_limit_bytes=VMEM_LIMIT)


def _sigmoid(x):
    return 1.0 / (1.0 + jnp.exp(-x))


def _silu(x):
    return x * _sigmoid(x)


def _softplus(x):
    return jnp.maximum(x, 0.0) + jnp.log(1.0 + jnp.exp(-jnp.abs(x)))


def _rms(x, g):
    ms = jnp.mean(x * x, axis=-1, keepdims=True)
    return x * lax.rsqrt(ms + NORM_EPS) * g


def _dot(a, b):
    return jnp.dot(a, b, preferred_element_type=F32)


def _dot_nt(a, b):
    return lax.dot_general(a, b, (((1,), (1,)), ((), ())), preferred_element_type=F32)


def _dot_tn(a, b):
    return lax.dot_general(a, b, (((0,), (0,)), ((), ())), preferred_element_type=F32)


def _bf16_terms(x, n):
    terms = []
    for _ in range(n):
        t = x.astype(BF16)
        terms.append(t)
        x = x - t.astype(F32)
    return terms


def _dot_exact_lhs(a, x, n=3):
    return sum(_dot(a, t) for t in _bf16_terms(x, n))


def _dot_exact_rhs(x, b, n=2):
    return sum(_dot(t, b) for t in _bf16_terms(x, n))


def _mod_kernel(c_ref, w_ref, b_ref, o_ref):
    act = _silu(c_ref[...]).astype(BF16)
    o_ref[0] = _dot(act, w_ref[0].astype(BF16)) + b_ref[0]


def _modulation(c8, w_mod, b_mod):
    depth = w_mod.shape[0]
    tn = 1536
    return pl.pallas_call(
        _mod_kernel,
        grid=(depth, (N_MOD * D_MODEL) // tn),
        in_specs=[pl.BlockSpec((8, D_MODEL), lambda l, j: (0, 0)),
                  pl.BlockSpec((1, D_MODEL, tn), lambda l, j: (l, 0, j)),
                  pl.BlockSpec((1, 1, tn), lambda l, j: (l, 0, j))],
        out_specs=pl.BlockSpec((1, 8, tn), lambda l, j: (l, 0, j)),
        out_shape=jax.ShapeDtypeStruct((depth, 8, N_MOD * D_MODEL), F32),
        compiler_params=_cparams(("arbitrary", "arbitrary")),
        name="modulation",
    )(c8, w_mod, b_mod.reshape(depth, 1, -1))


def _mod_row(i, tpb):
    b = i // tpb
    return jnp.where(i % tpb == tpb - 1, 2, b)


def _inproj_kernel(h_ref, mod_ref, g_ref, w_ref, z_ref, xbc_ref, gqa_ref, lru_ref, diff_ref, dt_ref):
    x = h_ref[...]
    u = _rms(x, g_ref[...]) * (1.0 + mod_ref[0, 1:2, :]) + mod_ref[0, 0:1, :]
    p = _dot(u.astype(BF16), w_ref[0])
    c0 = 0
    for ref in (z_ref, xbc_ref, gqa_ref, lru_ref, diff_ref, dt_ref):
        w = ref.shape[-1]
        ref[...] = p[:, c0:c0 + w]
        c0 += w


def _in_proj(h, mod3, g, w_in_p, layer, tpb):
    n = h.shape[0]
    widths = (GROUP_W, SSD_XBC, GQA_COLS, LRU_COLS, DIFF_COLS, DT_PAD)
    return pl.pallas_call(
        _inproj_kernel,
        grid=(n // ROW_TILE,),
        in_specs=[pl.BlockSpec((ROW_TILE, D_MODEL), lambda i: (i, 0)),
                  pl.BlockSpec((1, N_MOD, D_MODEL), lambda i: (_mod_row(i, tpb), 0, 0)),
                  pl.BlockSpec((1, D_MODEL), lambda i: (0, 0)),
                  pl.BlockSpec((1, D_MODEL, IN_COLS_P), lambda i: (layer, 0, 0))],
        out_specs=[pl.BlockSpec((ROW_TILE, w), lambda i: (i, 0)) for w in widths],
        out_shape=[jax.ShapeDtypeStruct((n, w), F32) for w in widths],
        compiler_params=_cparams(("arbitrary",)),
        name="in_proj",
    )(h, mod3, g, w_in_p)


def _conv_block(ext_ref, x, xp, xn, w_ref, b_ref, rows):
    ext_ref[0:HALO, :] = xp
    ext_ref[HALO:HALO + rows, :] = x
    ext_ref[HALO + rows:2 * HALO + rows, :] = xn
    acc = b_ref[...] + w_ref[0:1, :] * ext_ref[pl.ds(HALO - CONV_PAD_L, rows), :]
    for j in range(1, CONV_W):
        acc = acc + w_ref[j:j + 1, :] * ext_ref[pl.ds(HALO - CONV_PAD_L + j, rows), :]
    return acc


def _halo_specs(nb, rows, width, col_blk, blk_of_step, t):
    per = rows // HALO
    last = t // HALO - 1
    prev = pl.BlockSpec((nb, HALO, width), lambda s: (0, jnp.maximum(blk_of_step(s) * per - 1, 0), col_blk))
    nxt = pl.BlockSpec((nb, HALO, width), lambda s: (0, jnp.minimum((blk_of_step(s) + 1) * per, last), col_blk))
    return prev, nxt


def _ssd_kernel(rev, n_lat_chunks, *refs):
    if rev:
        xact_ref, dt_ref, z_ref, yf_ref, dtb_ref, a_ref, d_ref, ng_ref, o_ref, h_scr = refs
    else:
        (xbc_ref, xp_ref, xn_ref, dt_ref, cw_ref, cb_ref, dtb_ref, a_ref, d_ref,
         o_ref, xact_ref, h_scr, ext_scr) = refs
    nb = dt_ref.shape[0]
    q = SSD_CHUNK
    s = pl.program_id(0)
    n_steps = pl.num_programs(0)
    if rev:
        chunk = n_steps - 1 - s
    else:
        chunk = jnp.where(s < 2, n_lat_chunks + s, s - 2)
    seg_start = (chunk == 0) | (chunk == n_lat_chunks)
    seg_end = (chunk == n_lat_chunks - 1) | (chunk == n_lat_chunks + 1)

    @pl.when(s == 0)
    def _():
        h_scr[...] = jnp.zeros_like(h_scr)

    li = lax.broadcasted_iota(jnp.int32, (q, q), 0)
    si = lax.broadcasted_iota(jnp.int32, (q, q), 1)
    keep = (li <= si) if rev else (li >= si)
    tri = keep.astype(BF16)
    last = 0 if rev else q - 1
    col0 = SSD_HEADS if rev else 0

    for b in range(nb):
        if rev:
            xbc = xact_ref[b]
        else:
            xp = jnp.where(seg_start, 0.0, xp_ref[b])
            xn = jnp.where(seg_end, 0.0, xn_ref[b])
            xbc = _silu(_conv_block(ext_scr, xbc_ref[b], xp, xn, cw_ref, cb_ref, q))
            xact_ref[b] = xbc
        xs = xbc[:, 0:GROUP_W]
        sp = _softplus(dt_ref[b] + dtb_ref[...])
        acum = _dot_exact_lhs(tri, sp * a_ref[...])
        acum_t = acum.T
        ys = []
        for g in range(SSD_GROUPS):
            bm = xbc[:, GROUP_W + g * SSD_N:GROUP_W + (g + 1) * SSD_N]
            cm = xbc[:, GROUP_W + (SSD_GROUPS + g) * SSD_N:GROUP_W + (SSD_GROUPS + g + 1) * SSD_N]
            cb = _dot_nt(cm.astype(BF16), bm.astype(BF16))
            for hh in range(g * (SSD_HEADS // SSD_GROUPS), (g + 1) * (SSD_HEADS // SSD_GROUPS)):
                c = col0 + hh
                ac = acum[:, c:c + 1]
                ar = acum_t[c:c + 1, :]
                tot = acum[last:last + 1, c:c + 1]
                decay = jnp.exp(jnp.where(keep, ac - ar, -jnp.inf))
                xh = xs[:, hh * SSD_P:(hh + 1) * SSD_P]
                xdt = (xh * sp[:, c:c + 1]).astype(BF16)
                y = _dot((cb * decay).astype(BF16), xdt)
                hin = h_scr[b, hh]
                y = y + _dot((cm * jnp.exp(ac)).astype(BF16), hin.astype(BF16))
                bw = (bm * jnp.exp(tot - ac)).astype(BF16)
                h_scr[b, hh] = hin * jnp.exp(tot) + _dot_tn(bw, xdt)
                ys.append(y)
        y = jnp.concatenate(ys, axis=1) + d_ref[...] * xs
        if rev:
            y = (y + yf_ref[b]) * _silu(z_ref[b])
            half = GROUP_W // SSD_GROUPS
            y = jnp.concatenate([_rms(y[:, :half], ng_ref[:, :half]), _rms(y[:, half:], ng_ref[:, half:])], axis=1)
        o_ref[b] = y


def _ssd_direction(rev, xbc, dt, z, yf, cw, cb, dtb, a_row, d_row, ng, t):
    nb = xbc.shape[0]
    n_chunks = t // SSD_CHUNK
    n_lat_chunks = (t - CTX) // SSD_CHUNK

    def chunk_of(s):
        if rev:
            return n_chunks - 1 - s
        return jnp.where(s < 2, n_lat_chunks + s, s - 2)

    def tok(width):
        return pl.BlockSpec((nb, SSD_CHUNK, width), lambda s: (0, chunk_of(s), 0))

    def const(shape):
        return pl.BlockSpec(shape, lambda s: (0,) * len(shape))

    state = pltpu.VMEM((nb, SSD_HEADS, SSD_N, SSD_P), F32)
    small = [const((1, DT_PAD)), const((1, DT_PAD)), const((1, GROUP_W))]
    if rev:
        in_specs = [tok(SSD_XBC), tok(DT_PAD), tok(GROUP_W), tok(GROUP_W)] + small + [const((1, GROUP_W))]
        args = [xbc, dt, z, yf, dtb, a_row, d_row, ng]
        out_specs, out_shape, scratch = tok(GROUP_W), jax.ShapeDtypeStruct((nb, t, GROUP_W), F32), [state]
    else:
        prev, nxt = _halo_specs(nb, SSD_CHUNK, SSD_XBC, 0, chunk_of, t)
        in_specs = [tok(SSD_XBC), prev, nxt, tok(DT_PAD), const((CONV_W, SSD_XBC)), const((1, SSD_XBC))] + small
        args = [xbc, xbc, xbc, dt, cw, cb, dtb, a_row, d_row]
        out_specs = [tok(GROUP_W), tok(SSD_XBC)]
        out_shape = [jax.ShapeDtypeStruct((nb, t, GROUP_W), F32), jax.ShapeDtypeStruct((nb, t, SSD_XBC), F32)]
        scratch = [state, pltpu.VMEM((SSD_CHUNK + 2 * HALO, SSD_XBC), F32)]
    return pl.pallas_call(
        functools.partial(_ssd_kernel, rev, n_lat_chunks),
        grid=(n_chunks,),
        in_specs=in_specs,
        out_specs=out_specs,
        out_shape=out_shape,
        scratch_shapes=scratch,
        compiler_params=_cparams(("arbitrary",)),
        name="ssd_bwd" if rev else "ssd_fwd",
    )(*args)


def _lru_kernel(rev, n_lat_blocks, *refs):
    if rev:
        xc_ref, gate_ref, hf_ref, w_ref, bias_ref, lam_ref, o_ref, h_scr, a_scr, u_scr = refs
    else:
        (x_ref, xp_ref, xn_ref, cw_ref, cb_ref, w_ref, bias_ref, lam_ref,
         o_ref, xc_ref, h_scr, a_scr, u_scr, ext_scr) = refs
    nb = o_ref.shape[0]
    rows = LRU_BLOCK
    s = pl.program_id(0)
    n_steps = pl.num_programs(0)
    if rev:
        blk = n_steps - 1 - s
    else:
        blk = jnp.where(s == 0, n_lat_blocks, s - 1)
    seg_start = (blk == 0) | (blk == n_lat_blocks)
    seg_end = (blk == n_lat_blocks - 1) | (blk == n_lat_blocks)

    @pl.when(s == 0)
    def _():
        h_scr[...] = jnp.zeros_like(h_scr)

    sp_lam = _softplus(-lam_ref[...])
    for b in range(nb):
        if rev:
            xc = xc_ref[b]
        else:
            xp = jnp.where(seg_start, 0.0, xp_ref[b])
            xn = jnp.where(seg_end, 0.0, xn_ref[b])
            xc = _conv_block(ext_scr, x_ref[b], xp, xn, cw_ref, cb_ref, rows)
            xc_ref[b] = xc
        ri = _dot(xc.astype(BF16), w_ref[...]) + bias_ref[...]
        r = _sigmoid(ri[:, :GROUP_W])
        gi = _sigmoid(ri[:, GROUP_W:])
        log_a = -LRU_C * r * sp_lam
        a_scr[b] = jnp.exp(log_a)
        u_scr[b] = jnp.sqrt(1.0 - jnp.exp(2.0 * log_a)) * (gi * xc)

    row_id = lax.broadcasted_iota(jnp.int32, (8, GROUP_W), 0)

    def tile_scan(a, u):
        for d in (1, 2, 4):
            shift, valid = (8 - d, row_id < 8 - d) if rev else (d, row_id >= d)
            u = u + a * jnp.where(valid, pltpu.roll(u, shift, 0), 0.0)
            a = a * jnp.where(valid, pltpu.roll(a, shift, 0), 1.0)
        return a, u

    def step(g8, hs):
        hs = list(hs)
        r0 = pl.multiple_of(((rows // 8 - 1 - g8) if rev else g8) * 8, 8)
        for b in range(nb):
            a, u = tile_scan(a_scr[b, pl.ds(r0, 8), :], u_scr[b, pl.ds(r0, 8), :])
            h = u + a * hs[b]
            u_scr[b, pl.ds(r0, 8), :] = h
            hs[b] = h[0:1] if rev else h[7:8]
        return tuple(hs)

    hs = lax.fori_loop(0, rows // 8, step, tuple(h_scr[b] for b in range(nb)))
    for b in range(nb):
        h_scr[b] = hs[b]
        if rev:
            gt = gate_ref[b]
            gelu = 0.5 * gt * (1.0 + jnp.tanh(math.sqrt(2.0 / math.pi) * (gt + 0.044715 * gt * gt * gt)))
            o_ref[b] = gelu * (hf_ref[b] + u_scr[b])
        else:
            o_ref[b] = u_scr[b]


def _lru_direction(rev, lru, xc, hf, cw, cb, w_dir, bias_dir, lam_dir, t):
    nb = lru.shape[0]
    n_blocks = t // LRU_BLOCK
    n_lat_blocks = (t - CTX) // LRU_BLOCK

    def blk_of(s):
        if rev:
            return n_blocks - 1 - s
        return jnp.where(s == 0, n_lat_blocks, s - 1)

    def tok(col_blk):
        return pl.BlockSpec((nb, LRU_BLOCK, GROUP_W), lambda s: (0, blk_of(s), col_blk))

    def const(shape):
        return pl.BlockSpec(shape, lambda s: (0,) * len(shape))

    gate_w = [const((GROUP_W, 2 * GROUP_W)), const((1, 2 * GROUP_W)), const((1, GROUP_W))]
    scratch = [pltpu.VMEM((nb, 1, GROUP_W), F32), pltpu.VMEM((nb, LRU_BLOCK, GROUP_W), F32),
               pltpu.VMEM((nb, LRU_BLOCK, GROUP_W), F32)]
    out_tok = pl.BlockSpec((nb, LRU_BLOCK, GROUP_W), lambda s: (0, blk_of(s), 0))
    out_sds = jax.ShapeDtypeStruct((nb, t, GROUP_W), F32)
    if rev:
        in_specs = [out_tok, tok(0), out_tok] + gate_w
        args = [xc, lru, hf, w_dir, bias_dir, lam_dir]
        out_specs, out_shape = out_tok, out_sds
    else:
        prev, nxt = _halo_specs(nb, LRU_BLOCK, GROUP_W, 1, blk_of, t)
        in_specs = [tok(1), prev, nxt, const((CONV_W, GROUP_W)), const((1, GROUP_W))] + gate_w
        args = [lru, lru, lru, cw, cb, w_dir, bias_dir, lam_dir]
        out_specs, out_shape = [out_tok, out_tok], [out_sds, out_sds]
        scratch.append(pltpu.VMEM((LRU_BLOCK + 2 * HALO, GROUP_W), F32))
    return pl.pallas_call(
        functools.partial(_lru_kernel, rev, n_lat_blocks),
        grid=(n_blocks,),
        in_specs=in_specs,
        out_specs=out_specs,
        out_shape=out_shape,
        scratch_shapes=scratch,
        compiler_params=_cparams(("arbitrary",)),
        name="lru_bwd" if rev else "lru_fwd",
    )(*args)


def _seg_mean(x2, seg_ref):
    return _dot_exact_rhs(x2, seg_ref[...])


def _rope(x, cos, sin_lo, sin_hi, half):
    w = x.shape[-1]
    rep = w // cos.shape[-1]
    cos, sin_lo, sin_hi = (jnp.concatenate([t] * rep, axis=1) if rep > 1 else t for t in (cos, sin_lo, sin_hi))
    return x * cos + pltpu.roll(x, w - half, 1) * sin_lo + pltpu.roll(x, half, 1) * sin_hi


def _prep_kernel(gqa_ref, diff_ref, gtab_ref, dtab_ref, seg64_ref, seg32_ref, gq_g_ref, gk_g_ref,
                 dq_g_ref, dk_g_ref, gq_ref, gk_ref, gv_ref, dq_ref, dk_ref, dv_ref):
    def norm_rope(x, seg_ref, g, tab_ref, half, scale):
        w = x.shape[-1]
        xn = x * lax.rsqrt(_seg_mean(x * x, seg_ref)[:, :w] + NORM_EPS) * g[:, :w]
        xr = _rope(xn, tab_ref[0], tab_ref[1], tab_ref[2], half)
        return xr * scale if scale != 1.0 else xr

    def scatter(ref, x, n, d):
        for hh in range(n):
            ref[0, hh] = x[:, hh * d:(hh + 1) * d].astype(ref.dtype)

    def scatter_v(ref, x, n, d):
        ones = jnp.ones((x.shape[0], V_PAD - d), F32)
        for hh in range(n):
            ref[0, hh] = jnp.concatenate([x[:, hh * d:(hh + 1) * d], ones], axis=1).astype(ref.dtype)

    def scatter_t(ref, x, n, d):
        xt = x.T
        for hh in range(n):
            ref[0, hh, 0] = xt[hh * d:(hh + 1) * d, :].astype(ref.dtype)

    gqa = gqa_ref[...]
    kw = GQA_KV * GQA_D
    q = norm_rope(gqa[:, :GROUP_W], seg64_ref, gq_g_ref[...], gtab_ref, GQA_D // 4, GQA_D ** -0.5 * LOG2E)
    k = _pad_lanes_rope(gqa[:, GROUP_W:GROUP_W + kw], seg64_ref, gk_g_ref[...], gtab_ref, GQA_D // 4)
    scatter(gq_ref, q, GQA_HEADS, GQA_D)
    scatter_t(gk_ref, k, GQA_KV, GQA_D)
    scatter_v(gv_ref, gqa[:, GROUP_W + kw:], GQA_KV, GQA_D)
    diff = diff_ref[...]
    dq = norm_rope(diff[:, :GROUP_W], seg32_ref, dq_g_ref[...], dtab_ref, DIFF_QK // 4, DIFF_QK ** -0.5 * LOG2E)
    dk = norm_rope(diff[:, GROUP_W:2 * GROUP_W], seg32_ref, dk_g_ref[...], dtab_ref, DIFF_QK // 4, 1.0)
    scatter(dq_ref, dq, 2 * DIFF_HEADS, DIFF_QK)
    scatter_t(dk_ref, dk, 2 * DIFF_HEADS, DIFF_QK)
    scatter_v(dv_ref, diff[:, 2 * GROUP_W:], DIFF_HEADS, DIFF_V)


def _pad_lanes_rope(x, seg_ref, g, tab_ref, half):
    w = x.shape[-1]
    xn = x * lax.rsqrt(_dot_exact_rhs(x * x, seg_ref[:w, :w]) + NORM_EPS) * g[:, :w]
    return _rope(xn, tab_ref[0], tab_ref[1], tab_ref[2], half)


def _attn_prep(gqa, diff, gtab, dtab, seg64, seg32, gq_g, gk_g, dq_g, dk_g, nb, t):
    tpb = t // ROW_TILE

    def heads(n, d):
        return (pl.BlockSpec((1, n, ROW_TILE, d), lambda i: (i // tpb, 0, i % tpb, 0)),
                jax.ShapeDtypeStruct((nb, n, t, d), BF16))

    def heads_t(n, d):
        return (pl.BlockSpec((1, n, 1, d, ROW_TILE), lambda i: (i // tpb, 0, i % tpb, 0, 0)),
                jax.ShapeDtypeStruct((nb, n, tpb, d, ROW_TILE), BF16))

    outs = [heads(GQA_HEADS, GQA_D), heads_t(GQA_KV, GQA_D), heads(GQA_KV, V_PAD),
            heads(2 * DIFF_HEADS, DIFF_QK), heads_t(2 * DIFF_HEADS, DIFF_QK), heads(DIFF_HEADS, V_PAD)]

    def const(shape):
        return pl.BlockSpec(shape, lambda i: (0,) * len(shape))

    return pl.pallas_call(
        _prep_kernel,
        grid=(nb * tpb,),
        in_specs=[pl.BlockSpec((ROW_TILE, GQA_COLS), lambda i: (i, 0)),
                  pl.BlockSpec((ROW_TILE, DIFF_COLS), lambda i: (i, 0)),
                  pl.BlockSpec((3, ROW_TILE, 128), lambda i: (0, i % tpb, 0)),
                  pl.BlockSpec((3, ROW_TILE, 128), lambda i: (0, i % tpb, 0)),
                  const((GROUP_W, GROUP_W)), const((GROUP_W, GROUP_W)),
                  const((1, GROUP_W)), const((1, GROUP_W)), const((1, GROUP_W)), const((1, GROUP_W))],
        out_specs=[o[0] for o in outs],
        out_shape=[o[1] for o in outs],
        compiler_params=_cparams(("arbitrary",)),
        name="attn_prep",
    )(gqa, diff, gtab, dtab, seg64, seg32, gq_g, gk_g, dq_g, dk_g)


def _attn_kernel(diff_mode, post_scale, unroll, q_ref, kt_ref, v_ref, *rest):
    if diff_mode:
        lam_ref, sg_ref, o_ref, s_scr, m_scr, acc_scr = rest
    else:
        o_ref, s_scr, m_scr, acc_scr = rest
    tq = q_ref.shape[2]
    n_tiles = kt_ref.shape[2]
    n_units = v_ref.shape[1]
    dv = o_ref.shape[-1] // 2

    def over_tiles(ctx_only, fn):
        if ctx_only:
            fn([n_tiles - 1])
        else:
            def body(i, carry):
                fn([i * unroll + j for j in range(unroll)])
                return carry
            lax.fori_loop(0, n_tiles // unroll, body, 0)

    def run_unit(u, ctx_only):
        if diff_mode:
            blocks = [(j * tq, q_ref[0, 2 * u + j], 2 * u + j) for j in range(2)]
        else:
            blocks = [(0, q_ref[0].reshape(2 * tq, q_ref.shape[-1]), u)]
        m_scr[...] = jnp.full_like(m_scr, -jnp.inf)

        def scores(kts):
            for kt in kts:
                for r0, q, kh in blocks:
                    r = q.shape[0]
                    s = _dot(q, kt_ref[0, kh, kt])
                    s_scr[kt, r0:r0 + r, :] = s
                    m_scr[r0:r0 + r, :] = jnp.maximum(m_scr[r0:r0 + r, :], jnp.maximum(s[:, :128], s[:, 128:]))

        over_tiles(ctx_only, scores)
        m_scr[...] = jnp.broadcast_to(jnp.max(m_scr[...], axis=-1, keepdims=True), m_scr.shape)
        acc_scr[...] = jnp.zeros_like(acc_scr)

        def accumulate(kts):
            acc = None
            for kt in kts:
                m_rep = m_scr[...]
                p = jnp.exp2(s_scr[kt] - jnp.concatenate([m_rep, m_rep], axis=1))
                off = kt * KEY_TILE if isinstance(kt, int) else pl.multiple_of(kt * KEY_TILE, KEY_TILE)
                d = _dot(p.astype(BF16), v_ref[0, u, pl.ds(off, KEY_TILE), :])
                acc = d if acc is None else acc + d
            acc_scr[...] += acc

        over_tiles(ctx_only, accumulate)
        acc = acc_scr[...]
        return acc[:, :dv] / acc[:, dv:dv + 1]

    def finish(ctx_only):
        outs = []
        for u in range(n_units):
            o = run_unit(u, ctx_only)
            if diff_mode:
                o = o[:tq] - lam_ref[:, :o.shape[-1]] * o[tq:]
                outs.append(_rms(o, sg_ref[...]) * post_scale)
            else:
                outs += [o[:tq], o[tq:]]
        o_ref[0] = jnp.concatenate(outs, axis=1)

    is_ctx = pl.program_id(2) == pl.num_programs(2) - 1

    @pl.when(is_ctx)
    def _():
        finish(True)

    @pl.when(jnp.logical_not(is_ctx))
    def _():
        finish(False)


def _attention(diff_mode, q, kt, v, lam_row, subln_g, post_scale):
    nb, _, t, dqk = q.shape
    dv = GROUP_W // (GQA_HEADS if not diff_mode else DIFF_HEADS)
    n_tiles = kt.shape[2]
    n_units = 2 if diff_mode else 1
    n_groups = v.shape[1] // n_units
    qh = q.shape[1] // n_groups
    kh = kt.shape[1] // n_groups
    assert kt.shape[-1] == KEY_TILE
    unroll = max(u for u in range(1, ATT_MAX_UNROLL + 1) if n_tiles % u == 0)
    in_specs = [pl.BlockSpec((1, qh, ATT_TQ, dqk), lambda b, g, i: (b, g, i, 0)),
                pl.BlockSpec((1, kh, n_tiles, dqk, KEY_TILE), lambda b, g, i: (b, g, 0, 0, 0)),
                pl.BlockSpec((1, n_units, t, V_PAD), lambda b, g, i: (b, g, 0, 0))]
    args = [q, kt, v]
    if diff_mode:
        in_specs += [pl.BlockSpec((1, 128), lambda b, g, i: (0, 0)), pl.BlockSpec((1, dv), lambda b, g, i: (0, 0))]
        args += [lam_row, subln_g]
    return pl.pallas_call(
        functools.partial(_attn_kernel, diff_mode, post_scale, unroll),
        grid=(nb, n_groups, t // ATT_TQ),
        in_specs=in_specs,
        out_specs=pl.BlockSpec((1, ATT_TQ, 128), lambda b, g, i: (b, i, g)),
        out_shape=jax.ShapeDtypeStruct((nb, t, GROUP_W), F32),
        scratch_shapes=[pltpu.VMEM((n_tiles, 2 * ATT_TQ, KEY_TILE), F32), pltpu.VMEM((2 * ATT_TQ, 128), F32),
                        pltpu.VMEM((2 * ATT_TQ, V_PAD), F32)],
        compiler_params=_cparams(("arbitrary", "arbitrary", "arbitrary")),
        name="diff_attn" if diff_mode else "gqa_attn",
    )(*args)


def _outmlp_kernel(h_ref, ya_ref, yb_ref, yc_ref, yd_ref, mod_ref, g_ref, wo_ref, w1_ref, w2_ref, o_ref):
    mix = jnp.concatenate([ya_ref[...], yb_ref[...], yc_ref[...], yd_ref[...]], axis=1).astype(BF16)
    h1 = h_ref[...] + mod_ref[0, 2:3, :] * _dot(mix, wo_ref[0])
    v = _rms(h1, g_ref[...]) * (1.0 + mod_ref[0, 4:5, :]) + mod_ref[0, 3:4, :]
    u = jnp.maximum(_dot(v.astype(BF16), w1_ref[0]), 0.0)
    o_ref[...] = h1 + mod_ref[0, 5:6, :] * _dot((u * u).astype(BF16), w2_ref[0])


def _out_mlp(h, ya, yb, yc, yd, mod3, g2, w_out, w1, w2, layer, tpb, latent_only):
    n = h.shape[0]
    lat = tpb - 1
    n_tiles = (n // ROW_TILE) // tpb * lat if latent_only else n // ROW_TILE

    def src(i):
        return (i // lat) * tpb + i % lat if latent_only else i

    def mod_row(i):
        return i // lat if latent_only else _mod_row(i, tpb)

    def tok(w):
        return pl.BlockSpec((ROW_TILE, w), lambda i: (src(i), 0))

    def weight(k, m):
        return pl.BlockSpec((1, k, m), lambda i: (layer, 0, 0), pipeline_mode=pl.Buffered(1))

    return pl.pallas_call(
        _outmlp_kernel,
        grid=(n_tiles,),
        in_specs=[tok(D_MODEL), tok(GROUP_W), tok(GROUP_W), tok(GROUP_W), tok(GROUP_W),
                  pl.BlockSpec((1, N_MOD, D_MODEL), lambda i: (mod_row(i), 0, 0)),
                  pl.BlockSpec((1, D_MODEL), lambda i: (0, 0)),
                  weight(D_MODEL, D_MODEL), weight(D_MODEL, D_FF), weight(D_FF, D_MODEL)],
        out_specs=pl.BlockSpec((ROW_TILE, D_MODEL), lambda i: (i, 0)),
        out_shape=jax.ShapeDtypeStruct((n_tiles * ROW_TILE, D_MODEL), F32),
        compiler_params=_cparams(("arbitrary",)),
        name="out_mlp",
    )(h, ya, yb, yc, yd, mod3, g2, w_out, w1, w2)


def _rope_tables(s_len, head_dim, lanes=128):
    m = head_dim // 2
    half = m // 2
    lane = np.arange(lanes)
    d = lane % head_dim
    freq = ROPE_THETA ** (-(d % half).astype(np.float64) / half)
    t = np.arange(s_len)
    pos = np.where((d < m)[None, :], (t // GRID_W)[:, None], (t % GRID_W)[:, None]).astype(np.float32)
    ang = pos * freq.astype(np.float32)[None, :]
    cos, sin = np.cos(ang), np.sin(ang)
    low = ((d % m) < half)[None, :]
    tab = np.stack([cos, np.where(low, -sin, 0.0), np.where(low, 0.0, sin)])
    ident = np.stack([np.ones((CTX, lanes)), np.zeros((CTX, lanes)), np.zeros((CTX, lanes))])
    return jnp.asarray(np.concatenate([tab, ident], axis=1), F32)


def _seg_matrix(width, seg):
    idx = np.arange(width) // seg
    return jnp.asarray((idx[:, None] == idx[None, :]).astype(np.float32) / seg, BF16)


def _lane_tile(v, width):
    return jnp.tile(v, width // v.shape[-1]).reshape(1, width)


def _block_diag(w):
    nblk, j, k = w.shape
    eye = jnp.eye(nblk, dtype=w.dtype)
    return jnp.einsum('njk,nm->njmk', w, eye).reshape(nblk * j, nblk * k)


def kernel(x, c, ctx, c_ctx, w_mod, b_mod, norm1_g, w_in, ssd_conv_w, ssd_conv_b, ssd_a_log, ssd_dt_bias, ssd_d, ssd_norm_g, gqa_q_norm_g, gqa_k_norm_g, lru_conv_w, lru_conv_b, lru_w_r, lru_b_r, lru_w_i, lru_b_i, lru_lambda, diff_q_norm_g, diff_k_norm_g, diff_lambda_q1, diff_lambda_k1, diff_lambda_q2, diff_lambda_k2, diff_subln_g, w_out, norm2_g, w_mlp1, w_mlp2):
    nb, s_len, d_model = x.shape
    depth = w_mod.shape[0]
    assert d_model == D_MODEL and ctx.shape[1] == CTX and s_len % ROW_TILE == 0 and s_len % GRID_W == 0
    t = s_len + CTX
    tpb = t // ROW_TILE
    n = nb * t

    ssd_cols = GROUP_W + SSD_XBC + 2 * SSD_HEADS
    w_in_p = jnp.concatenate(
        [w_in[:, :, :GROUP_W + SSD_XBC], w_in[:, :, ssd_cols:], w_in[:, :, GROUP_W + SSD_XBC:ssd_cols],
         jnp.zeros((depth, D_MODEL, DT_PAD - 2 * SSD_HEADS), w_in.dtype)], axis=-1).astype(BF16)
    w_out_b, w1_b, w2_b = w_out.astype(BF16), w_mlp1.astype(BF16), w_mlp2.astype(BF16)
    lru_w = jnp.stack([jnp.concatenate([_block_diag(lru_w_r[l, d]), _block_diag(lru_w_i[l, d])], axis=1)
                       for l in range(depth) for d in range(2)]).reshape(depth, 2, GROUP_W, 2 * GROUP_W).astype(BF16)
    lru_b = jnp.concatenate([lru_b_r, lru_b_i], axis=-1)
    pad8 = DT_PAD - 2 * SSD_HEADS
    dtb_rows = jnp.pad(ssd_dt_bias.reshape(depth, 1, 2 * SSD_HEADS), ((0, 0), (0, 0), (0, pad8)))
    a_rows = jnp.pad(-jnp.exp(ssd_a_log.reshape(depth, 1, 2 * SSD_HEADS)), ((0, 0), (0, 0), (0, pad8)))
    d_rows = jnp.repeat(ssd_d, SSD_P, axis=-1)
    lam_diff = (jnp.exp(jnp.sum(diff_lambda_q1 * diff_lambda_k1, axis=-1))
                - jnp.exp(jnp.sum(diff_lambda_q2 * diff_lambda_k2, axis=-1)))
    gtab = _rope_tables(s_len, GQA_D)
    dtab = _rope_tables(s_len, DIFF_QK)
    seg64 = _seg_matrix(GROUP_W, GQA_D)
    seg32 = _seg_matrix(GROUP_W, DIFF_QK)

    c8 = jnp.concatenate([c, c_ctx[None, :], jnp.zeros((8 - nb - 1, D_MODEL), F32)], axis=0)
    mod = _modulation(c8, w_mod, b_mod)
    h = jnp.concatenate([x, ctx], axis=1).reshape(n, D_MODEL)

    for l in range(depth):
        lam_init = 0.8 - 0.6 * math.exp(-0.3 * l)
        mod3 = mod[l].reshape(8, N_MOD, D_MODEL)
        z, xbc, gqa, lru, diff, dt = _in_proj(h, mod3, norm1_g[l].reshape(1, -1), w_in_p, l, tpb)
        r3 = lambda a: a.reshape(nb, t, a.shape[-1])
        ssd_args = (r3(xbc), r3(dt), r3(z))
        cw, cb = ssd_conv_w[l], ssd_conv_b[l].reshape(1, -1)
        ya_f, xact = _ssd_direction(False, *ssd_args, None, cw, cb, dtb_rows[l], a_rows[l], d_rows[l, 0:1], None, t)
        ya = _ssd_direction(True, xact, *ssd_args[1:], ya_f, None, None, dtb_rows[l], a_rows[l], d_rows[l, 1:2],
                            ssd_norm_g[l].reshape(1, -1), t)
        lcw, lcb = lru_conv_w[l], lru_conv_b[l].reshape(1, -1)
        hf, xc = _lru_direction(False, r3(lru), None, None, lcw, lcb, lru_w[l, 0], lru_b[l, 0:1],
                                lru_lambda[l, 0:1], t)
        yc = _lru_direction(True, r3(lru), xc, hf, None, None, lru_w[l, 1], lru_b[l, 1:2], lru_lambda[l, 1:2], t)
        gq, gk, gv, dq, dk, dv = _attn_prep(
            gqa, diff, gtab, dtab, seg64, seg32, _lane_tile(gqa_q_norm_g[l], GROUP_W),
            _lane_tile(gqa_k_norm_g[l], GROUP_W), _lane_tile(diff_q_norm_g[l], GROUP_W),
            _lane_tile(diff_k_norm_g[l], GROUP_W), nb, t)
        last = l == depth - 1
        yb = _attention(False, gq, gk, gv, None, None, 1.0)
        lam_row = jnp.full((1, 128), lam_init, F32) + lam_diff[l]
        yd = _attention(True, dq, dk, dv, lam_row, diff_subln_g[l].reshape(1, -1), 1.0 - lam_init)
        h = _out_mlp(h, ya.reshape(n, -1), yb.reshape(n, -1), yc.reshape(n, -1), yd.reshape(n, -1), mod3,
                     norm2_g[l].reshape(1, -1), w_out_b, w1_b, w2_b, l, tpb, last)
    return h.reshape(nb, s_len, D_MODEL)
```

```python
import functools
import math

import jax
import jax.numpy as jnp
import numpy as np
from jax import lax
from jax.experimental import pallas as pl
from jax.experimental.pallas import tpu as pltpu

F32 = jnp.float32
BF16 = jnp.bfloat16

D_MODEL = 1024
CTX = 256
GROUP_W = 256
D_FF = 4 * D_MODEL
N_MOD = 6
NORM_EPS = 1e-6
ROPE_THETA = 10000.0
GRID_W = 64
CONV_W = 4
CONV_PAD_L = CONV_W // 2
HALO = 8

SSD_HEADS = 4
SSD_P = 64
SSD_GROUPS = 2
SSD_N = 128
SSD_CHUNK = 128
SSD_XBC = GROUP_W + 2 * SSD_GROUPS * SSD_N
GQA_HEADS = 4
GQA_KV = 2
GQA_D = 64
GQA_COLS = GROUP_W + 2 * GQA_KV * GQA_D
LRU_C = 8.0
LRU_COLS = 2 * GROUP_W
DIFF_HEADS = 4
DIFF_V = 64
DIFF_QK = 32
DIFF_COLS = 3 * GROUP_W
DT_PAD = 128
ATT_COLS = GQA_COLS + DIFF_COLS
IN_COLS_P = ATT_COLS + GROUP_W + SSD_XBC + LRU_COLS + DT_PAD

ROW_TILE = 256
LRU_BLOCK = 256
ATT_TQ = 256
KEY_TILE = 256
ATT_MAX_UNROLL = 33
V_PAD = 128
LOG2E = math.log2(math.e)
VMEM_LIMIT = 56 * 1024 * 1024


def _cparams(sem):
    return pltpu.CompilerParams(dimension_semantics=sem, vmem_limit_bytes=VMEM_LIMIT)


def _sigmoid(x):
    return 1.0 / (1.0 + jnp.exp(-x))


def _silu(x):
    return x * _sigmoid(x)


def _softplus(x):
    return jnp.maximum(x, 0.0) + jnp.log(1.0 + jnp.exp(-jnp.abs(x)))


def _rms(x, g):
    ms = jnp.mean(x * x, axis=-1, keepdims=True)
    return x * lax.rsqrt(ms + NORM_EPS) * g


def _dot(a, b):
    return jnp.dot(a, b, preferred_element_type=F32)


def _dot_nt(a, b):
    return lax.dot_general(a, b, (((1,), (1,)), ((), ())), preferred_element_type=F32)


def _dot_tn(a, b):
    return lax.dot_general(a, b, (((0,), (0,)), ((), ())), preferred_element_type=F32)


def _bf16_terms(x, n):
    terms = []
    for _ in range(n):
        t = x.astype(BF16)
        terms.append(t)
        x = x - t.astype(F32)
    return terms


def _dot_exact_lhs(a, x, n=3):
    return sum(_dot(a, t) for t in _bf16_terms(x, n))


def _dot_exact_rhs(x, b, n=2):
    return sum(_dot(t, b) for t in _bf16_terms(x, n))


def _mod_kernel(c_ref, w_ref, b_ref, o_ref):
    act = _silu(c_ref[...]).astype(BF16)
    o_ref[0] = _dot(act, w_ref[0].astype(BF16)) + b_ref[0]


def _modulation(c8, w_mod, b_mod):
    depth = w_mod.shape[0]
    tn = 1536
    return pl.pallas_call(
        _mod_kernel,
        grid=(depth, (N_MOD * D_MODEL) // tn),
        in_specs=[pl.BlockSpec((8, D_MODEL), lambda l, j: (0, 0)),
                  pl.BlockSpec((1, D_MODEL, tn), lambda l, j: (l, 0, j)),
                  pl.BlockSpec((1, 1, tn), lambda l, j: (l, 0, j))],
        out_specs=pl.BlockSpec((1, 8, tn), lambda l, j: (l, 0, j)),
        out_shape=jax.ShapeDtypeStruct((depth, 8, N_MOD * D_MODEL), F32),
        compiler_params=_cparams(("arbitrary", "arbitrary")),
        name="modulation",
    )(c8, w_mod, b_mod.reshape(depth, 1, -1))


def _mod_row(i, tpb):
    b = i // tpb
    return jnp.where(i % tpb == tpb - 1, 2, b)


def _inproj_kernel(h_ref, mod_ref, g_ref, w_ref, gtab_ref, dtab_ref, seg64_ref, seg32_ref, gq_g_ref, gk_g_ref,
                   dq_g_ref, dk_g_ref, z_ref, xbc_ref, lru_ref, dt_ref, gq_ref, gk_ref, gv_ref, dq_ref, dk_ref,
                   dv_ref):
    x = h_ref[...]
    u = (_rms(x, g_ref[...]) * (1.0 + mod_ref[0, 1:2, :]) + mod_ref[0, 0:1, :]).astype(BF16)
    att = _dot(u, w_ref[0, :, :ATT_COLS])
    _qkv_prep(att[:, :GQA_COLS], att[:, GQA_COLS:], gtab_ref, dtab_ref, seg64_ref, seg32_ref, gq_g_ref, gk_g_ref,
              dq_g_ref, dk_g_ref, gq_ref, gk_ref, gv_ref, dq_ref, dk_ref, dv_ref)
    p = _dot(u, w_ref[0, :, ATT_COLS:])
    c0 = 0
    for ref in (z_ref, xbc_ref, lru_ref, dt_ref):
        w = ref.shape[-1]
        ref[...] = p[:, c0:c0 + w]
        c0 += w


def _in_proj(h, mod3, g, w_in_p, layer, gtab, dtab, seg64, seg32, gq_g, gk_g, dq_g, dk_g, nb, t):
    n = h.shape[0]
    tpb = t // ROW_TILE
    widths = (GROUP_W, SSD_XBC, LRU_COLS, DT_PAD)

    def heads(nh, d):
        return (pl.BlockSpec((1, nh, ROW_TILE, d), lambda i: (i // tpb, 0, i % tpb, 0)),
                jax.ShapeDtypeStruct((nb, nh, t, d), BF16))

    def heads_t(nh, d):
        return (pl.BlockSpec((1, nh, 1, d, ROW_TILE), lambda i: (i // tpb, 0, i % tpb, 0, 0)),
                jax.ShapeDtypeStruct((nb, nh, tpb, d, ROW_TILE), BF16))

    outs = [(pl.BlockSpec((ROW_TILE, w), lambda i: (i, 0)), jax.ShapeDtypeStruct((n, w), F32)) for w in widths]
    outs += [heads(GQA_HEADS, GQA_D), heads_t(GQA_KV, GQA_D), heads(GQA_KV, V_PAD),
             heads(2 * DIFF_HEADS, DIFF_QK), heads_t(2 * DIFF_HEADS, DIFF_QK), heads(DIFF_HEADS, V_PAD)]

    def const(shape):
        return pl.BlockSpec(shape, lambda i: (0,) * len(shape))

    return pl.pallas_call(
        _inproj_kernel,
        grid=(n // ROW_TILE,),
        in_specs=[pl.BlockSpec((ROW_TILE, D_MODEL), lambda i: (i, 0)),
                  pl.BlockSpec((1, N_MOD, D_MODEL), lambda i: (_mod_row(i, tpb), 0, 0)),
                  pl.BlockSpec((1, D_MODEL), lambda i: (0, 0)),
                  pl.BlockSpec((1, D_MODEL, IN_COLS_P), lambda i: (layer, 0, 0)),
                  pl.BlockSpec((3, ROW_TILE, 128), lambda i: (0, i % tpb, 0)),
                  pl.BlockSpec((3, ROW_TILE, 128), lambda i: (0, i % tpb, 0)),
                  const((GROUP_W, GROUP_W)), const((GROUP_W, GROUP_W)),
                  const((1, GROUP_W)), const((1, GROUP_W)), const((1, GROUP_W)), const((1, GROUP_W))],
        out_specs=[o[0] for o in outs],
        out_shape=[o[1] for o in outs],
        compiler_params=_cparams(("arbitrary",)),
        name="in_proj",
    )(h, mod3, g, w_in_p, gtab, dtab, seg64, seg32, gq_g, gk_g, dq_g, dk_g)


def _conv_block(ext_ref, x, xp, xn, w_ref, b_ref, rows):
    ext_ref[0:HALO, :] = xp
    ext_ref[HALO:HALO + rows, :] = x
    ext_ref[HALO + rows:2 * HALO + rows, :] = xn
    acc = b_ref[...] + w_ref[0:1, :] * ext_ref[pl.ds(HALO - CONV_PAD_L, rows), :]
    for j in range(1, CONV_W):
        acc = acc + w_ref[j:j + 1, :] * ext_ref[pl.ds(HALO - CONV_PAD_L + j, rows), :]
    return acc


def _halo_specs(nb, rows, width, col_blk, blk_of_step, t):
    per = rows // HALO
    last = t // HALO - 1
    prev = pl.BlockSpec((nb, HALO, width), lambda s: (0, jnp.maximum(blk_of_step(s) * per - 1, 0), col_blk))
    nxt = pl.BlockSpec((nb, HALO, width), lambda s: (0, jnp.minimum((blk_of_step(s) + 1) * per, last), col_blk))
    return prev, nxt


def _ssd_kernel(rev, n_lat_chunks, *refs):
    if rev:
        xact_ref, dt_ref, z_ref, yf_ref, dtb_ref, a_ref, d_ref, ng_ref, o_ref, h_scr = refs
    else:
        (xbc_ref, xp_ref, xn_ref, dt_ref, cw_ref, cb_ref, dtb_ref, a_ref, d_ref,
         o_ref, xact_ref, h_scr, ext_scr) = refs
    nb = dt_ref.shape[0]
    q = SSD_CHUNK
    s = pl.program_id(0)
    n_steps = pl.num_programs(0)
    if rev:
        chunk = n_steps - 1 - s
    else:
        chunk = jnp.where(s < 2, n_lat_chunks + s, s - 2)
    seg_start = (chunk == 0) | (chunk == n_lat_chunks)
    seg_end = (chunk == n_lat_chunks - 1) | (chunk == n_lat_chunks + 1)

    @pl.when(s == 0)
    def _():
        h_scr[...] = jnp.zeros_like(h_scr)

    li = lax.broadcasted_iota(jnp.int32, (q, q), 0)
    si = lax.broadcasted_iota(jnp.int32, (q, q), 1)
    keep = (li <= si) if rev else (li >= si)
    tri = keep.astype(BF16)
    last = 0 if rev else q - 1
    col0 = SSD_HEADS if rev else 0

    for b in range(nb):
        if rev:
            xbc = xact_ref[b]
        else:
            xp = jnp.where(seg_start, 0.0, xp_ref[b])
            xn = jnp.where(seg_end, 0.0, xn_ref[b])
            xbc = _silu(_conv_block(ext_scr, xbc_ref[b], xp, xn, cw_ref, cb_ref, q))
            xact_ref[b] = xbc
        xs = xbc[:, 0:GROUP_W]
        sp = _softplus(dt_ref[b] + dtb_ref[...])
        acum = _dot_exact_lhs(tri, sp * a_ref[...])
        acum_t = acum.T
        ys = []
        for g in range(SSD_GROUPS):
            bm = xbc[:, GROUP_W + g * SSD_N:GROUP_W + (g + 1) * SSD_N]
            cm = xbc[:, GROUP_W + (SSD_GROUPS + g) * SSD_N:GROUP_W + (SSD_GROUPS + g + 1) * SSD_N]
            cb = _dot_nt(cm.astype(BF16), bm.astype(BF16))
            for hh in range(g * (SSD_HEADS // SSD_GROUPS), (g + 1) * (SSD_HEADS // SSD_GROUPS)):
                c = col0 + hh
                ac = acum[:, c:c + 1]
                ar = acum_t[c:c + 1, :]
                tot = acum[last:last + 1, c:c + 1]
                decay = jnp.exp(jnp.where(keep, ac - ar, -jnp.inf))
                xh = xs[:, hh * SSD_P:(hh + 1) * SSD_P]
                xdt = (xh * sp[:, c:c + 1]).astype(BF16)
                y = _dot((cb * decay).astype(BF16), xdt)
                hin = h_scr[b, hh]
                y = y + _dot((cm * jnp.exp(ac)).astype(BF16), hin.astype(BF16))
                bw = (bm * jnp.exp(tot - ac)).astype(BF16)
                h_scr[b, hh] = hin * jnp.exp(tot) + _dot_tn(bw, xdt)
                ys.append(y)
        y = jnp.concatenate(ys, axis=1) + d_ref[...] * xs
        if rev:
            y = (y + yf_ref[b]) * _silu(z_ref[b])
            half = GROUP_W // SSD_GROUPS
            y = jnp.concatenate([_rms(y[:, :half], ng_ref[:, :half]), _rms(y[:, half:], ng_ref[:, half:])], axis=1)
        o_ref[b] = y


def _ssd_direction(rev, xbc, dt, z, yf, cw, cb, dtb, a_row, d_row, ng, t):
    nb = xbc.shape[0]
    n_chunks = t // SSD_CHUNK
    n_lat_chunks = (t - CTX) // SSD_CHUNK

    def chunk_of(s):
        if rev:
            return n_chunks - 1 - s
        return jnp.where(s < 2, n_lat_chunks + s, s - 2)

    def tok(width):
        return pl.BlockSpec((nb, SSD_CHUNK, width), lambda s: (0, chunk_of(s), 0))

    def const(shape):
        return pl.BlockSpec(shape, lambda s: (0,) * len(shape))

    state = pltpu.VMEM((nb, SSD_HEADS, SSD_N, SSD_P), F32)
    small = [const((1, DT_PAD)), const((1, DT_PAD)), const((1, GROUP_W))]
    if rev:
        in_specs = [tok(SSD_XBC), tok(DT_PAD), tok(GROUP_W), tok(GROUP_W)] + small + [const((1, GROUP_W))]
        args = [xbc, dt, z, yf, dtb, a_row, d_row, ng]
        out_specs, out_shape, scratch = tok(GROUP_W), jax.ShapeDtypeStruct((nb, t, GROUP_W), F32), [state]
    else:
        prev, nxt = _halo_specs(nb, SSD_CHUNK, SSD_XBC, 0, chunk_of, t)
        in_specs = [tok(SSD_XBC), prev, nxt, tok(DT_PAD), const((CONV_W, SSD_XBC)), const((1, SSD_XBC))] + small
        args = [xbc, xbc, xbc, dt, cw, cb, dtb, a_row, d_row]
        out_specs = [tok(GROUP_W), tok(SSD_XBC)]
        out_shape = [jax.ShapeDtypeStruct((nb, t, GROUP_W), F32), jax.ShapeDtypeStruct((nb, t, SSD_XBC), F32)]
        scratch = [state, pltpu.VMEM((SSD_CHUNK + 2 * HALO, SSD_XBC), F32)]
    return pl.pallas_call(
        functools.partial(_ssd_kernel, rev, n_lat_chunks),
        grid=(n_chunks,),
        in_specs=in_specs,
        out_specs=out_specs,
        out_shape=out_shape,
        scratch_shapes=scratch,
        compiler_params=_cparams(("arbitrary",)),
        name="ssd_bwd" if rev else "ssd_fwd",
    )(*args)


def _lru_kernel(rev, n_lat_blocks, *refs):
    if rev:
        xc_ref, gate_ref, hf_ref, w_ref, bias_ref, lam_ref, o_ref, h_scr, a_scr, u_scr = refs
    else:
        (x_ref, xp_ref, xn_ref, cw_ref, cb_ref, w_ref, bias_ref, lam_ref,
         o_ref, xc_ref, h_scr, a_scr, u_scr, ext_scr) = refs
    nb = o_ref.shape[0]
    rows = LRU_BLOCK
    s = pl.program_id(0)
    n_steps = pl.num_programs(0)
    if rev:
        blk = n_steps - 1 - s
    else:
        blk = jnp.where(s == 0, n_lat_blocks, s - 1)
    seg_start = (blk == 0) | (blk == n_lat_blocks)
    seg_end = (blk == n_lat_blocks - 1) | (blk == n_lat_blocks)

    @pl.when(s == 0)
    def _():
        h_scr[...] = jnp.zeros_like(h_scr)

    sp_lam = _softplus(-lam_ref[...])
    for b in range(nb):
        if rev:
            xc = xc_ref[b]
        else:
            xp = jnp.where(seg_start, 0.0, xp_ref[b])
            xn = jnp.where(seg_end, 0.0, xn_ref[b])
            xc = _conv_block(ext_scr, x_ref[b], xp, xn, cw_ref, cb_ref, rows)
            xc_ref[b] = xc
        ri = _dot(xc.astype(BF16), w_ref[...]) + bias_ref[...]
        r = _sigmoid(ri[:, :GROUP_W])
        gi = _sigmoid(ri[:, GROUP_W:])
        log_a = -LRU_C * r * sp_lam
        a_scr[b] = jnp.exp(log_a)
        u_scr[b] = jnp.sqrt(1.0 - jnp.exp(2.0 * log_a)) * (gi * xc)

    row_id = lax.broadcasted_iota(jnp.int32, (8, GROUP_W), 0)

    def tile_scan(a, u):
        for d in (1, 2, 4):
            shift, valid = (8 - d, row_id < 8 - d) if rev else (d, row_id >= d)
            u = u + a * jnp.where(valid, pltpu.roll(u, shift, 0), 0.0)
            a = a * jnp.where(valid, pltpu.roll(a, shift, 0), 1.0)
        return a, u

    def step(g8, hs):
        hs = list(hs)
        r0 = pl.multiple_of(((rows // 8 - 1 - g8) if rev else g8) * 8, 8)
        for b in range(nb):
            a, u = tile_scan(a_scr[b, pl.ds(r0, 8), :], u_scr[b, pl.ds(r0, 8), :])
            h = u + a * hs[b]
            u_scr[b, pl.ds(r0, 8), :] = h
            hs[b] = h[0:1] if rev else h[7:8]
        return tuple(hs)

    hs = lax.fori_loop(0, rows // 8, step, tuple(h_scr[b] for b in range(nb)))
    for b in range(nb):
        h_scr[b] = hs[b]
        if rev:
            gt = gate_ref[b]
            gelu = 0.5 * gt * (1.0 + jnp.tanh(math.sqrt(2.0 / math.pi) * (gt + 0.044715 * gt * gt * gt)))
            o_ref[b] = gelu * (hf_ref[b] + u_scr[b])
        else:
            o_ref[b] = u_scr[b]


def _lru_direction(rev, lru, xc, hf, cw, cb, w_dir, bias_dir, lam_dir, t):
    nb = lru.shape[0]
    n_blocks = t // LRU_BLOCK
    n_lat_blocks = (t - CTX) // LRU_BLOCK

    def blk_of(s):
        if rev:
            return n_blocks - 1 - s
        return jnp.where(s == 0, n_lat_blocks, s - 1)

    def tok(col_blk):
        return pl.BlockSpec((nb, LRU_BLOCK, GROUP_W), lambda s: (0, blk_of(s), col_blk))

    def const(shape):
        return pl.BlockSpec(shape, lambda s: (0,) * len(shape))

    gate_w = [const((GROUP_W, 2 * GROUP_W)), const((1, 2 * GROUP_W)), const((1, GROUP_W))]
    scratch = [pltpu.VMEM((nb, 1, GROUP_W), F32), pltpu.VMEM((nb, LRU_BLOCK, GROUP_W), F32),
               pltpu.VMEM((nb, LRU_BLOCK, GROUP_W), F32)]
    out_tok = pl.BlockSpec((nb, LRU_BLOCK, GROUP_W), lambda s: (0, blk_of(s), 0))
    out_sds = jax.ShapeDtypeStruct((nb, t, GROUP_W), F32)
    if rev:
        in_specs = [out_tok, tok(0), out_tok] + gate_w
        args = [xc, lru, hf, w_dir, bias_dir, lam_dir]
        out_specs, out_shape = out_tok, out_sds
    else:
        prev, nxt = _halo_specs(nb, LRU_BLOCK, GROUP_W, 1, blk_of, t)
        in_specs = [tok(1), prev, nxt, const((CONV_W, GROUP_W)), const((1, GROUP_W))] + gate_w
        args = [lru, lru, lru, cw, cb, w_dir, bias_dir, lam_dir]
        out_specs, out_shape = [out_tok, out_tok], [out_sds, out_sds]
        scratch.append(pltpu.VMEM((LRU_BLOCK + 2 * HALO, GROUP_W), F32))
    return pl.pallas_call(
        functools.partial(_lru_kernel, rev, n_lat_blocks),
        grid=(n_blocks,),
        in_specs=in_specs,
        out_specs=out_specs,
        out_shape=out_shape,
        scratch_shapes=scratch,
        compiler_params=_cparams(("arbitrary",)),
        name="lru_bwd" if rev else "lru_fwd",
    )(*args)


def _seg_mean(x2, seg_ref):
    return _dot_exact_rhs(x2, seg_ref[...])


def _rope(x, cos, sin_lo, sin_hi, half):
    w = x.shape[-1]
    rep = w // cos.shape[-1]
    cos, sin_lo, sin_hi = (jnp.concatenate([t] * rep, axis=1) if rep > 1 else t for t in (cos, sin_lo, sin_hi))
    return x * cos + pltpu.roll(x, w - half, 1) * sin_lo + pltpu.roll(x, half, 1) * sin_hi


def _qkv_prep(gqa, diff, gtab_ref, dtab_ref, seg64_ref, seg32_ref, gq_g_ref, gk_g_ref,
              dq_g_ref, dk_g_ref, gq_ref, gk_ref, gv_ref, dq_ref, dk_ref, dv_ref):
    def norm_rope(x, seg_ref, g, tab_ref, half, scale):
        w = x.shape[-1]
        xn = x * lax.rsqrt(_seg_mean(x * x, seg_ref)[:, :w] + NORM_EPS) * g[:, :w]
        xr = _rope(xn, tab_ref[0], tab_ref[1], tab_ref[2], half)
        return xr * scale if scale != 1.0 else xr

    def scatter(ref, x, n, d):
        for hh in range(n):
            ref[0, hh] = x[:, hh * d:(hh + 1) * d].astype(ref.dtype)

    def scatter_v(ref, x, n, d):
        ones = jnp.ones((x.shape[0], V_PAD - d), F32)
        for hh in range(n):
            ref[0, hh] = jnp.concatenate([x[:, hh * d:(hh + 1) * d], ones], axis=1).astype(ref.dtype)

    def scatter_t(ref, x, n, d):
        xt = x.T
        for hh in range(n):
            ref[0, hh, 0] = xt[hh * d:(hh + 1) * d, :].astype(ref.dtype)

    kw = GQA_KV * GQA_D
    q = norm_rope(gqa[:, :GROUP_W], seg64_ref, gq_g_ref[...], gtab_ref, GQA_D // 4, GQA_D ** -0.5 * LOG2E)
    k = _pad_lanes_rope(gqa[:, GROUP_W:GROUP_W + kw], seg64_ref, gk_g_ref[...], gtab_ref, GQA_D // 4)
    scatter(gq_ref, q, GQA_HEADS, GQA_D)
    scatter_t(gk_ref, k, GQA_KV, GQA_D)
    scatter_v(gv_ref, gqa[:, GROUP_W + kw:], GQA_KV, GQA_D)
    dq = norm_rope(diff[:, :GROUP_W], seg32_ref, dq_g_ref[...], dtab_ref, DIFF_QK // 4, DIFF_QK ** -0.5 * LOG2E)
    dk = norm_rope(diff[:, GROUP_W:2 * GROUP_W], seg32_ref, dk_g_ref[...], dtab_ref, DIFF_QK // 4, 1.0)
    scatter(dq_ref, dq, 2 * DIFF_HEADS, DIFF_QK)
    scatter_t(dk_ref, dk, 2 * DIFF_HEADS, DIFF_QK)
    scatter_v(dv_ref, diff[:, 2 * GROUP_W:], DIFF_HEADS, DIFF_V)


def _pad_lanes_rope(x, seg_ref, g, tab_ref, half):
    w = x.shape[-1]
    xn = x * lax.rsqrt(_dot_exact_rhs(x * x, seg_ref[:w, :w]) + NORM_EPS) * g[:, :w]
    return _rope(xn, tab_ref[0], tab_ref[1], tab_ref[2], half)


def _attn_kernel(diff_mode, post_scale, unroll, q_ref, kt_ref, v_ref, *rest):
    if diff_mode:
        lam_ref, sg_ref, o_ref, s_scr, m_scr, acc_scr = rest
    else:
        o_ref, s_scr, m_scr, acc_scr = rest
    tq = q_ref.shape[2]
    n_tiles = kt_ref.shape[2]
    n_units = v_ref.shape[1]
    dv = o_ref.shape[-1] // 2

    def over_tiles(ctx_only, fn):
        if ctx_only:
            fn([n_tiles - 1])
        else:
            def body(i, carry):
                fn([i * unroll + j for j in range(unroll)])
                return carry
            lax.fori_loop(0, n_tiles // unroll, body, 0)

    def run_unit(u, ctx_only):
        if diff_mode:
            blocks = [(j * tq, q_ref[0, 2 * u + j], 2 * u + j) for j in range(2)]
        else:
            blocks = [(0, q_ref[0].reshape(2 * tq, q_ref.shape[-1]), u)]
        m_scr[...] = jnp.full_like(m_scr, -jnp.inf)

        def scores(kts):
            for kt in kts:
                for r0, q, kh in blocks:
                    r = q.shape[0]
                    s = _dot(q, kt_ref[0, kh, kt])
                    s_scr[kt, r0:r0 + r, :] = s
                    m_scr[r0:r0 + r, :] = jnp.maximum(m_scr[r0:r0 + r, :], jnp.maximum(s[:, :128], s[:, 128:]))

        over_tiles(ctx_only, scores)
        m_scr[...] = jnp.broadcast_to(jnp.max(m_scr[...], axis=-1, keepdims=True), m_scr.shape)
        acc_scr[...] = jnp.zeros_like(acc_scr)

        def accumulate(kts):
            acc = None
            for kt in kts:
                m_rep = m_scr[...]
                p = jnp.exp2(s_scr[kt] - jnp.concatenate([m_rep, m_rep], axis=1))
                off = kt * KEY_TILE if isinstance(kt, int) else pl.multiple_of(kt * KEY_TILE, KEY_TILE)
                d = _dot(p.astype(BF16), v_ref[0, u, pl.ds(off, KEY_TILE), :])
                acc = d if acc is None else acc + d
            acc_scr[...] += acc

        over_tiles(ctx_only, accumulate)
        acc = acc_scr[...]
        return acc[:, :dv] / acc[:, dv:dv + 1]

    def finish(ctx_only):
        outs = []
        for u in range(n_units):
            o = run_unit(u, ctx_only)
            if diff_mode:
                o = o[:tq] - lam_ref[:, :o.shape[-1]] * o[tq:]
                outs.append(_rms(o, sg_ref[...]) * post_scale)
            else:
                outs += [o[:tq], o[tq:]]
        o_ref[0] = jnp.concatenate(outs, axis=1)

    is_ctx = pl.program_id(2) == pl.num_programs(2) - 1

    @pl.when(is_ctx)
    def _():
        finish(True)

    @pl.when(jnp.logical_not(is_ctx))
    def _():
        finish(False)


def _attention(diff_mode, q, kt, v, lam_row, subln_g, post_scale):
    nb, _, t, dqk = q.shape
    dv = GROUP_W // (GQA_HEADS if not diff_mode else DIFF_HEADS)
    n_tiles = kt.shape[2]
    n_units = 2 if diff_mode else 1
    n_groups = v.shape[1] // n_units
    qh = q.shape[1] // n_groups
    kh = kt.shape[1] // n_groups
    assert kt.shape[-1] == KEY_TILE
    unroll = max(u for u in range(1, ATT_MAX_UNROLL + 1) if n_tiles % u == 0)
    in_specs = [pl.BlockSpec((1, qh, ATT_TQ, dqk), lambda b, g, i: (b, g, i, 0)),
                pl.BlockSpec((1, kh, n_tiles, dqk, KEY_TILE), lambda b, g, i: (b, g, 0, 0, 0)),
                pl.BlockSpec((1, n_units, t, V_PAD), lambda b, g, i: (b, g, 0, 0))]
    args = [q, kt, v]
    if diff_mode:
        in_specs += [pl.BlockSpec((1, 128), lambda b, g, i: (0, 0)), pl.BlockSpec((1, dv), lambda b, g, i: (0, 0))]
        args += [lam_row, subln_g]
    return pl.pallas_call(
        functools.partial(_attn_kernel, diff_mode, post_scale, unroll),
        grid=(nb, n_groups, t // ATT_TQ),
        in_specs=in_specs,
        out_specs=pl.BlockSpec((1, ATT_TQ, 128), lambda b, g, i: (b, i, g)),
        out_shape=jax.ShapeDtypeStruct((nb, t, GROUP_W), F32),
        scratch_shapes=[pltpu.VMEM((n_tiles, 2 * ATT_TQ, KEY_TILE), F32), pltpu.VMEM((2 * ATT_TQ, 128), F32),
                        pltpu.VMEM((2 * ATT_TQ, V_PAD), F32)],
        compiler_params=_cparams(("arbitrary", "arbitrary", "arbitrary")),
        name="diff_attn" if diff_mode else "gqa_attn",
    )(*args)


def _outmlp_kernel(h_ref, ya_ref, yb_ref, yc_ref, yd_ref, mod_ref, g_ref, wo_ref, w1_ref, w2_ref, o_ref):
    mix = jnp.concatenate([ya_ref[...], yb_ref[...], yc_ref[...], yd_ref[...]], axis=1).astype(BF16)
    h1 = h_ref[...] + mod_ref[0, 2:3, :] * _dot(mix, wo_ref[0])
    v = _rms(h1, g_ref[...]) * (1.0 + mod_ref[0, 4:5, :]) + mod_ref[0, 3:4, :]
    u = jnp.maximum(_dot(v.astype(BF16), w1_ref[0]), 0.0)
    o_ref[...] = h1 + mod_ref[0, 5:6, :] * _dot((u * u).astype(BF16), w2_ref[0])


def _out_mlp(h, ya, yb, yc, yd, mod3, g2, w_out, w1, w2, layer, tpb, latent_only):
    n = h.shape[0]
    lat = tpb - 1
    n_tiles = (n // ROW_TILE) // tpb * lat if latent_only else n // ROW_TILE

    def src(i):
        return (i // lat) * tpb + i % lat if latent_only else i

    def mod_row(i):
        return i // lat if latent_only else _mod_row(i, tpb)

    def tok(w):
        return pl.BlockSpec((ROW_TILE, w), lambda i: (src(i), 0))

    def weight(k, m):
        return pl.BlockSpec((1, k, m), lambda i: (layer, 0, 0), pipeline_mode=pl.Buffered(1))

    return pl.pallas_call(
        _outmlp_kernel,
        grid=(n_tiles,),
        in_specs=[tok(D_MODEL), tok(GROUP_W), tok(GROUP_W), tok(GROUP_W), tok(GROUP_W),
                  pl.BlockSpec((1, N_MOD, D_MODEL), lambda i: (mod_row(i), 0, 0)),
                  pl.BlockSpec((1, D_MODEL), lambda i: (0, 0)),
                  weight(D_MODEL, D_MODEL), weight(D_MODEL, D_FF), weight(D_FF, D_MODEL)],
        out_specs=pl.BlockSpec((ROW_TILE, D_MODEL), lambda i: (i, 0)),
        out_shape=jax.ShapeDtypeStruct((n_tiles * ROW_TILE, D_MODEL), F32),
        compiler_params=_cparams(("arbitrary",)),
        name="out_mlp",
    )(h, ya, yb, yc, yd, mod3, g2, w_out, w1, w2)


def _rope_tables(s_len, head_dim, lanes=128):
    m = head_dim // 2
    half = m // 2
    lane = np.arange(lanes)
    d = lane % head_dim
    freq = ROPE_THETA ** (-(d % half).astype(np.float64) / half)
    t = np.arange(s_len)
    pos = np.where((d < m)[None, :], (t // GRID_W)[:, None], (t % GRID_W)[:, None]).astype(np.float32)
    ang = pos * freq.astype(np.float32)[None, :]
    cos, sin = np.cos(ang), np.sin(ang)
    low = ((d % m) < half)[None, :]
    tab = np.stack([cos, np.where(low, -sin, 0.0), np.where(low, 0.0, sin)])
    ident = np.stack([np.ones((CTX, lanes)), np.zeros((CTX, lanes)), np.zeros((CTX, lanes))])
    return jnp.asarray(np.concatenate([tab, ident], axis=1), F32)


def _seg_matrix(width, seg):
    idx = np.arange(width) // seg
    return jnp.asarray((idx[:, None] == idx[None, :]).astype(np.float32) / seg, BF16)


def _lane_tile(v, width):
    return jnp.tile(v, width // v.shape[-1]).reshape(1, width)


def _block_diag(w):
    nblk, j, k = w.shape
    eye = jnp.eye(nblk, dtype=w.dtype)
    return jnp.einsum('njk,nm->njmk', w, eye).reshape(nblk * j, nblk * k)


def kernel(x, c, ctx, c_ctx, w_mod, b_mod, norm1_g, w_in, ssd_conv_w, ssd_conv_b, ssd_a_log, ssd_dt_bias, ssd_d, ssd_norm_g, gqa_q_norm_g, gqa_k_norm_g, lru_conv_w, lru_conv_b, lru_w_r, lru_b_r, lru_w_i, lru_b_i, lru_lambda, diff_q_norm_g, diff_k_norm_g, diff_lambda_q1, diff_lambda_k1, diff_lambda_q2, diff_lambda_k2, diff_subln_g, w_out, norm2_g, w_mlp1, w_mlp2):
    nb, s_len, d_model = x.shape
    depth = w_mod.shape[0]
    assert d_model == D_MODEL and ctx.shape[1] == CTX and s_len % ROW_TILE == 0 and s_len % GRID_W == 0
    t = s_len + CTX
    tpb = t // ROW_TILE
    n = nb * t

    zx, dt_end = GROUP_W + SSD_XBC, GROUP_W + SSD_XBC + 2 * SSD_HEADS
    gqa_end, lru_end = dt_end + GQA_COLS, dt_end + GQA_COLS + LRU_COLS
    w_in_p = jnp.concatenate(
        [w_in[:, :, dt_end:gqa_end], w_in[:, :, lru_end:], w_in[:, :, :zx], w_in[:, :, gqa_end:lru_end],
         w_in[:, :, zx:dt_end], jnp.zeros((depth, D_MODEL, DT_PAD - 2 * SSD_HEADS), w_in.dtype)],
        axis=-1).astype(BF16)
    w_out_b, w1_b, w2_b = w_out.astype(BF16), w_mlp1.astype(BF16), w_mlp2.astype(BF16)
    lru_w = jnp.stack([jnp.concatenate([_block_diag(lru_w_r[l, d]), _block_diag(lru_w_i[l, d])], axis=1)
                       for l in range(depth) for d in range(2)]).reshape(depth, 2, GROUP_W, 2 * GROUP_W).astype(BF16)
    lru_b = jnp.concatenate([lru_b_r, lru_b_i], axis=-1)
    pad8 = DT_PAD - 2 * SSD_HEADS
    dtb_rows = jnp.pad(ssd_dt_bias.reshape(depth, 1, 2 * SSD_HEADS), ((0, 0), (0, 0), (0, pad8)))
    a_rows = jnp.pad(-jnp.exp(ssd_a_log.reshape(depth, 1, 2 * SSD_HEADS)), ((0, 0), (0, 0), (0, pad8)))
    d_rows = jnp.repeat(ssd_d, SSD_P, axis=-1)
    lam_diff = (jnp.exp(jnp.sum(diff_lambda_q1 * diff_lambda_k1, axis=-1))
                - jnp.exp(jnp.sum(diff_lambda_q2 * diff_lambda_k2, axis=-1)))
    gtab = _rope_tables(s_len, GQA_D)
    dtab = _rope_tables(s_len, DIFF_QK)
    seg64 = _seg_matrix(GROUP_W, GQA_D)
    seg32 = _seg_matrix(GROUP_W, DIFF_QK)

    c8 = jnp.concatenate([c, c_ctx[None, :], jnp.zeros((8 - nb - 1, D_MODEL), F32)], axis=0)
    mod = _modulation(c8, w_mod, b_mod)
    h = jnp.concatenate([x, ctx], axis=1).reshape(n, D_MODEL)

    for l in range(depth):
        lam_init = 0.8 - 0.6 * math.exp(-0.3 * l)
        mod3 = mod[l].reshape(8, N_MOD, D_MODEL)
        z, xbc, lru, dt, gq, gk, gv, dq, dk, dv = _in_proj(
            h, mod3, norm1_g[l].reshape(1, -1), w_in_p, l, gtab, dtab, seg64, seg32,
            _lane_tile(gqa_q_norm_g[l], GROUP_W), _lane_tile(gqa_k_norm_g[l], GROUP_W),
            _lane_tile(diff_q_norm_g[l], GROUP_W), _lane_tile(diff_k_norm_g[l], GROUP_W), nb, t)
        r3 = lambda a: a.reshape(nb, t, a.shape[-1])
        ssd_args = (r3(xbc), r3(dt), r3(z))
        cw, cb = ssd_conv_w[l], ssd_conv_b[l].reshape(1, -1)
        ya_f, xact = _ssd_direction(False, *ssd_args, None, cw, cb, dtb_rows[l], a_rows[l], d_rows[l, 0:1], None, t)
        ya = _ssd_direction(True, xact, *ssd_args[1:], ya_f, None, None, dtb_rows[l], a_rows[l], d_rows[l, 1:2],
                            ssd_norm_g[l].reshape(1, -1), t)
        lcw, lcb = lru_conv_w[l], lru_conv_b[l].reshape(1, -1)
        hf, xc = _lru_direction(False, r3(lru), None, None, lcw, lcb, lru_w[l, 0], lru_b[l, 0:1],
                                lru_lambda[l, 0:1], t)
        yc = _lru_direction(True, r3(lru), xc, hf, None, None, lru_w[l, 1], lru_b[l, 1:2], lru_lambda[l, 1:2], t)
        last = l == depth - 1
        yb = _attention(False, gq, gk, gv, None, None, 1.0)
        lam_row = jnp.full((1, 128), lam_init, F32) + lam_diff[l]
        yd = _attention(True, dq, dk, dv, lam_row, diff_subln_g[l].reshape(1, -1), 1.0 - lam_init)
        h = _out_mlp(h, ya.reshape(n, -1), yb.reshape(n, -1), yc.reshape(n, -1), yd.reshape(n, -1), mod3,
                     norm2_g[l].reshape(1, -1), w_out_b, w1_b, w2_b, l, tpb, last)
    return h.reshape(nb, s_len, D_MODEL)
```

```python
import functools
import math

import jax
import jax.numpy as jnp
import numpy as np
from jax import lax
from jax.experimental import pallas as pl
from jax.experimental.pallas import tpu as pltpu

F32 = jnp.float32
BF16 = jnp.bfloat16

D_MODEL = 1024
CTX = 256
GROUP_W = 256
D_FF = 4 * D_MODEL
N_MOD = 6
NORM_EPS = 1e-6
ROPE_THETA = 10000.0
GRID_W = 64
CONV_W = 4
CONV_PAD_L = CONV_W // 2
HALO = 8

SSD_HEADS = 4
SSD_P = 64
SSD_GROUPS = 2
SSD_N = 128
SSD_CHUNK = 128
SSD_XBC = GROUP_W + 2 * SSD_GROUPS * SSD_N
GQA_HEADS = 4
GQA_KV = 2
GQA_D = 64
GQA_COLS = GROUP_W + 2 * GQA_KV * GQA_D
LRU_C = 8.0
LRU_COLS = 2 * GROUP_W
DIFF_HEADS = 4
DIFF_V = 64
DIFF_QK = 32
DIFF_COLS = 3 * GROUP_W
DT_PAD = 128
ATT_COLS = GQA_COLS + DIFF_COLS
IN_COLS_P = ATT_COLS + GROUP_W + SSD_XBC + LRU_COLS + DT_PAD

ROW_TILE = 256
LRU_BLOCK = 256
ATT_TQ = 256
KEY_TILE = 256
ATT_MAX_UNROLL = 33
V_PAD = 128
LOG2E = math.log2(math.e)
VMEM_LIMIT = 56 * 1024 * 1024


def _cparams(sem):
    return pltpu.CompilerParams(dimension_semantics=sem, vmem_limit_bytes=VMEM_LIMIT)


def _sigmoid(x):
    return 1.0 / (1.0 + jnp.exp(-x))


def _silu(x):
    return x * _sigmoid(x)


def _softplus(x):
    return jnp.maximum(x, 0.0) + jnp.log(1.0 + jnp.exp(-jnp.abs(x)))


def _rms(x, g):
    ms = jnp.mean(x * x, axis=-1, keepdims=True)
    return x * lax.rsqrt(ms + NORM_EPS) * g


def _dot(a, b):
    return jnp.dot(a, b, preferred_element_type=F32)


def _dot_nt(a, b):
    return lax.dot_general(a, b, (((1,), (1,)), ((), ())), preferred_element_type=F32)


def _dot_tn(a, b):
    return lax.dot_general(a, b, (((0,), (0,)), ((), ())), preferred_element_type=F32)


def _bf16_terms(x, n):
    terms = []
    for _ in range(n):
        t = x.astype(BF16)
        terms.append(t)
        x = x - t.astype(F32)
    return terms


def _dot_exact_lhs(a, x, n=3):
    return sum(_dot(a, t) for t in _bf16_terms(x, n))


def _dot_exact_rhs(x, b, n=2):
    return sum(_dot(t, b) for t in _bf16_terms(x, n))


def _mod_kernel(c_ref, w_ref, b_ref, o_ref):
    act = _silu(c_ref[...]).astype(BF16)
    o_ref[0] = _dot(act, w_ref[0].astype(BF16)) + b_ref[0]


def _modulation(c8, w_mod, b_mod):
    depth = w_mod.shape[0]
    tn = 1536
    return pl.pallas_call(
        _mod_kernel,
        grid=(depth, (N_MOD * D_MODEL) // tn),
        in_specs=[pl.BlockSpec((8, D_MODEL), lambda l, j: (0, 0)),
                  pl.BlockSpec((1, D_MODEL, tn), lambda l, j: (l, 0, j)),
                  pl.BlockSpec((1, 1, tn), lambda l, j: (l, 0, j))],
        out_specs=pl.BlockSpec((1, 8, tn), lambda l, j: (l, 0, j)),
        out_shape=jax.ShapeDtypeStruct((depth, 8, N_MOD * D_MODEL), F32),
        compiler_params=_cparams(("arbitrary", "arbitrary")),
        name="modulation",
    )(c8, w_mod, b_mod.reshape(depth, 1, -1))


def _mod_row(i, tpb):
    b = i // tpb
    return jnp.where(i % tpb == tpb - 1, 2, b)


def _inproj_kernel(h_ref, mod_ref, g_ref, w_ref, gtab_ref, dtab_ref, seg64_ref, seg32_ref, gq_g_ref, gk_g_ref,
                   dq_g_ref, dk_g_ref, z_ref, xbc_ref, lru_ref, dt_ref, gq_ref, gk_ref, gv_ref, dq_ref, dk_ref,
                   dv_ref):
    x = h_ref[...]
    u = (_rms(x, g_ref[...]) * (1.0 + mod_ref[0, 1:2, :]) + mod_ref[0, 0:1, :]).astype(BF16)
    att = _dot(u, w_ref[0, :, :ATT_COLS])
    _qkv_prep(att[:, :GQA_COLS], att[:, GQA_COLS:], gtab_ref, dtab_ref, seg64_ref, seg32_ref, gq_g_ref, gk_g_ref,
              dq_g_ref, dk_g_ref, gq_ref, gk_ref, gv_ref, dq_ref, dk_ref, dv_ref)
    p = _dot(u, w_ref[0, :, ATT_COLS:])
    c0 = 0
    for ref in (z_ref, xbc_ref, lru_ref, dt_ref):
        w = ref.shape[-1]
        ref[...] = p[:, c0:c0 + w]
        c0 += w


def _in_proj(h, mod3, g, w_in_p, layer, gtab, dtab, seg64, seg32, gq_g, gk_g, dq_g, dk_g, nb, t):
    n = h.shape[0]
    tpb = t // ROW_TILE
    widths = (GROUP_W, SSD_XBC, LRU_COLS, DT_PAD)

    def heads(nh, d):
        return (pl.BlockSpec((1, nh, ROW_TILE, d), lambda i: (i // tpb, 0, i % tpb, 0)),
                jax.ShapeDtypeStruct((nb, nh, t, d), BF16))

    def heads_t(nh, d):
        return (pl.BlockSpec((1, nh, 1, d, ROW_TILE), lambda i: (i // tpb, 0, i % tpb, 0, 0)),
                jax.ShapeDtypeStruct((nb, nh, tpb, d, ROW_TILE), BF16))

    outs = [(pl.BlockSpec((ROW_TILE, w), lambda i: (i, 0)), jax.ShapeDtypeStruct((n, w), F32)) for w in widths]
    outs += [heads(GQA_HEADS, GQA_D), heads_t(GQA_KV, GQA_D), heads(GQA_KV, V_PAD),
             heads(2 * DIFF_HEADS, DIFF_QK), heads_t(2 * DIFF_HEADS, DIFF_QK), heads(DIFF_HEADS, V_PAD)]

    def const(shape):
        return pl.BlockSpec(shape, lambda i: (0,) * len(shape))

    return pl.pallas_call(
        _inproj_kernel,
        grid=(n // ROW_TILE,),
        in_specs=[pl.BlockSpec((ROW_TILE, D_MODEL), lambda i: (i, 0)),
                  pl.BlockSpec((1, N_MOD, D_MODEL), lambda i: (_mod_row(i, tpb), 0, 0)),
                  pl.BlockSpec((1, D_MODEL), lambda i: (0, 0)),
                  pl.BlockSpec((1, D_MODEL, IN_COLS_P), lambda i: (layer, 0, 0)),
                  pl.BlockSpec((3, ROW_TILE, 128), lambda i: (0, i % tpb, 0)),
                  pl.BlockSpec((3, ROW_TILE, 128), lambda i: (0, i % tpb, 0)),
                  const((GROUP_W, GROUP_W)), const((GROUP_W, GROUP_W)),
                  const((1, GROUP_W)), const((1, GROUP_W)), const((1, GROUP_W)), const((1, GROUP_W))],
        out_specs=[o[0] for o in outs],
        out_shape=[o[1] for o in outs],
        compiler_params=_cparams(("arbitrary",)),
        name="in_proj",
    )(h, mod3, g, w_in_p, gtab, dtab, seg64, seg32, gq_g, gk_g, dq_g, dk_g)


def _conv_block(ext_ref, x, xp, xn, w_ref, b_ref, rows):
    ext_ref[0:HALO, :] = xp
    ext_ref[HALO:HALO + rows, :] = x
    ext_ref[HALO + rows:2 * HALO + rows, :] = xn
    acc = b_ref[...] + w_ref[0:1, :] * ext_ref[pl.ds(HALO - CONV_PAD_L, rows), :]
    for j in range(1, CONV_W):
        acc = acc + w_ref[j:j + 1, :] * ext_ref[pl.ds(HALO - CONV_PAD_L + j, rows), :]
    return acc


def _halo_specs(nb, rows, width, col_blk, blk_of_step, t):
    per = rows // HALO
    last = t // HALO - 1
    prev = pl.BlockSpec((nb, HALO, width), lambda s: (0, jnp.maximum(blk_of_step(s) * per - 1, 0), col_blk))
    nxt = pl.BlockSpec((nb, HALO, width), lambda s: (0, jnp.minimum((blk_of_step(s) + 1) * per, last), col_blk))
    return prev, nxt


def _ssd_kernel(rev, n_lat_chunks, *refs):
    if rev:
        xact_ref, dt_ref, z_ref, yf_ref, dtb_ref, a_ref, d_ref, ng_ref, o_ref, h_scr = refs
    else:
        (xbc_ref, xp_ref, xn_ref, dt_ref, cw_ref, cb_ref, dtb_ref, a_ref, d_ref,
         o_ref, xact_ref, h_scr, ext_scr) = refs
    nb = dt_ref.shape[0]
    q = SSD_CHUNK
    s = pl.program_id(0)
    n_steps = pl.num_programs(0)
    if rev:
        chunk = n_steps - 1 - s
    else:
        chunk = jnp.where(s < 2, n_lat_chunks + s, s - 2)
    seg_start = (chunk == 0) | (chunk == n_lat_chunks)
    seg_end = (chunk == n_lat_chunks - 1) | (chunk == n_lat_chunks + 1)

    @pl.when(s == 0)
    def _():
        h_scr[...] = jnp.zeros_like(h_scr)

    li = lax.broadcasted_iota(jnp.int32, (q, q), 0)
    si = lax.broadcasted_iota(jnp.int32, (q, q), 1)
    keep = (li <= si) if rev else (li >= si)
    tri = keep.astype(BF16)
    last = 0 if rev else q - 1
    col0 = SSD_HEADS if rev else 0

    pre = []
    for b in range(nb):
        if rev:
            xbc = xact_ref[b]
        else:
            xp = jnp.where(seg_start, 0.0, xp_ref[b])
            xn = jnp.where(seg_end, 0.0, xn_ref[b])
            xbc = _silu(_conv_block(ext_scr, xbc_ref[b], xp, xn, cw_ref, cb_ref, q))
            xact_ref[b] = xbc
        sp = _softplus(dt_ref[b] + dtb_ref[...])
        acum = _dot_exact_lhs(tri, sp * a_ref[...])
        pre.append((xbc, sp, acum, acum.T))
    ys = [[] for _ in range(nb)]
    for g in range(SSD_GROUPS):
        grp = []
        for b in range(nb):
            xbc = pre[b][0]
            bm = xbc[:, GROUP_W + g * SSD_N:GROUP_W + (g + 1) * SSD_N]
            cm = xbc[:, GROUP_W + (SSD_GROUPS + g) * SSD_N:GROUP_W + (SSD_GROUPS + g + 1) * SSD_N]
            grp.append((bm.T.astype(BF16), cm, _dot_nt(cm.astype(BF16), bm.astype(BF16))))
        for hh in range(g * (SSD_HEADS // SSD_GROUPS), (g + 1) * (SSD_HEADS // SSD_GROUPS)):
            c = col0 + hh
            for b in range(nb):
                xbc, sp, acum, acum_t = pre[b]
                bm_t, cm, cb = grp[b]
                ac = acum[:, c:c + 1]
                ar = acum_t[c:c + 1, :]
                tot = acum[last:last + 1, c:c + 1]
                decay = jnp.exp(jnp.where(keep, ac - ar, -jnp.inf))
                xdt = xbc[:, hh * SSD_P:(hh + 1) * SSD_P] * sp[:, c:c + 1]
                y = _dot((cb * decay).astype(BF16), xdt.astype(BF16))
                hin = h_scr[b, hh]
                y = y + _dot((cm * jnp.exp(ac)).astype(BF16), hin.astype(BF16))
                h_scr[b, hh] = hin * jnp.exp(tot) + _dot(bm_t, (xdt * jnp.exp(tot - ac)).astype(BF16))
                ys[b].append(y)
    for b in range(nb):
        y = jnp.concatenate(ys[b], axis=1) + d_ref[...] * pre[b][0][:, 0:GROUP_W]
        if rev:
            y = (y + yf_ref[b]) * _silu(z_ref[b])
            half = GROUP_W // SSD_GROUPS
            y = jnp.concatenate([_rms(y[:, :half], ng_ref[:, :half]), _rms(y[:, half:], ng_ref[:, half:])], axis=1)
        o_ref[b] = y


def _ssd_direction(rev, xbc, dt, z, yf, cw, cb, dtb, a_row, d_row, ng, t):
    nb = xbc.shape[0]
    n_chunks = t // SSD_CHUNK
    n_lat_chunks = (t - CTX) // SSD_CHUNK

    def chunk_of(s):
        if rev:
            return n_chunks - 1 - s
        return jnp.where(s < 2, n_lat_chunks + s, s - 2)

    def tok(width):
        return pl.BlockSpec((nb, SSD_CHUNK, width), lambda s: (0, chunk_of(s), 0))

    def const(shape):
        return pl.BlockSpec(shape, lambda s: (0,) * len(shape))

    state = pltpu.VMEM((nb, SSD_HEADS, SSD_N, SSD_P), F32)
    small = [const((1, DT_PAD)), const((1, DT_PAD)), const((1, GROUP_W))]
    if rev:
        in_specs = [tok(SSD_XBC), tok(DT_PAD), tok(GROUP_W), tok(GROUP_W)] + small + [const((1, GROUP_W))]
        args = [xbc, dt, z, yf, dtb, a_row, d_row, ng]
        out_specs, out_shape, scratch = tok(GROUP_W), jax.ShapeDtypeStruct((nb, t, GROUP_W), F32), [state]
    else:
        prev, nxt = _halo_specs(nb, SSD_CHUNK, SSD_XBC, 0, chunk_of, t)
        in_specs = [tok(SSD_XBC), prev, nxt, tok(DT_PAD), const((CONV_W, SSD_XBC)), const((1, SSD_XBC))] + small
        args = [xbc, xbc, xbc, dt, cw, cb, dtb, a_row, d_row]
        out_specs = [tok(GROUP_W), tok(SSD_XBC)]
        out_shape = [jax.ShapeDtypeStruct((nb, t, GROUP_W), F32), jax.ShapeDtypeStruct((nb, t, SSD_XBC), F32)]
        scratch = [state, pltpu.VMEM((SSD_CHUNK + 2 * HALO, SSD_XBC), F32)]
    return pl.pallas_call(
        functools.partial(_ssd_kernel, rev, n_lat_chunks),
        grid=(n_chunks,),
        in_specs=in_specs,
        out_specs=out_specs,
        out_shape=out_shape,
        scratch_shapes=scratch,
        compiler_params=_cparams(("arbitrary",)),
        name="ssd_bwd" if rev else "ssd_fwd",
    )(*args)


def _lru_kernel(rev, n_lat_blocks, *refs):
    if rev:
        xc_ref, gate_ref, hf_ref, w_ref, bias_ref, lam_ref, o_ref, h_scr, a_scr, u_scr = refs
    else:
        (x_ref, xp_ref, xn_ref, cw_ref, cb_ref, w_ref, bias_ref, lam_ref,
         o_ref, xc_ref, h_scr, a_scr, u_scr, ext_scr) = refs
    nb = o_ref.shape[0]
    rows = LRU_BLOCK
    s = pl.program_id(0)
    n_steps = pl.num_programs(0)
    if rev:
        blk = n_steps - 1 - s
    else:
        blk = jnp.where(s == 0, n_lat_blocks, s - 1)
    seg_start = (blk == 0) | (blk == n_lat_blocks)
    seg_end = (blk == n_lat_blocks - 1) | (blk == n_lat_blocks)

    @pl.when(s == 0)
    def _():
        h_scr[...] = jnp.zeros_like(h_scr)

    sp_lam = _softplus(-lam_ref[...])
    for b in range(nb):
        if rev:
            xc = xc_ref[b]
        else:
            xp = jnp.where(seg_start, 0.0, xp_ref[b])
            xn = jnp.where(seg_end, 0.0, xn_ref[b])
            xc = _conv_block(ext_scr, x_ref[b], xp, xn, cw_ref, cb_ref, rows)
            xc_ref[b] = xc
        ri = _dot(xc.astype(BF16), w_ref[...]) + bias_ref[...]
        r = _sigmoid(ri[:, :GROUP_W])
        gi = _sigmoid(ri[:, GROUP_W:])
        log_a = -LRU_C * r * sp_lam
        a_scr[b] = jnp.exp(log_a)
        u_scr[b] = jnp.sqrt(1.0 - jnp.exp(2.0 * log_a)) * (gi * xc)

    row_id = lax.broadcasted_iota(jnp.int32, (8, GROUP_W), 0)

    def tile_scan(a, u):
        for d in (1, 2, 4):
            shift, valid = (8 - d, row_id < 8 - d) if rev else (d, row_id >= d)
            u = u + a * jnp.where(valid, pltpu.roll(u, shift, 0), 0.0)
            a = a * jnp.where(valid, pltpu.roll(a, shift, 0), 1.0)
        return a, u

    def step(g8, hs):
        hs = list(hs)
        r0 = pl.multiple_of(((rows // 8 - 1 - g8) if rev else g8) * 8, 8)
        for b in range(nb):
            a, u = tile_scan(a_scr[b, pl.ds(r0, 8), :], u_scr[b, pl.ds(r0, 8), :])
            h = u + a * hs[b]
            u_scr[b, pl.ds(r0, 8), :] = h
            hs[b] = h[0:1] if rev else h[7:8]
        return tuple(hs)

    hs = lax.fori_loop(0, rows // 8, step, tuple(h_scr[b] for b in range(nb)))
    for b in range(nb):
        h_scr[b] = hs[b]
        if rev:
            gt = gate_ref[b]
            gelu = 0.5 * gt * (1.0 + jnp.tanh(math.sqrt(2.0 / math.pi) * (gt + 0.044715 * gt * gt * gt)))
            o_ref[b] = gelu * (hf_ref[b] + u_scr[b])
        else:
            o_ref[b] = u_scr[b]


def _lru_direction(rev, lru, xc, hf, cw, cb, w_dir, bias_dir, lam_dir, t):
    nb = lru.shape[0]
    n_blocks = t // LRU_BLOCK
    n_lat_blocks = (t - CTX) // LRU_BLOCK

    def blk_of(s):
        if rev:
            return n_blocks - 1 - s
        return jnp.where(s == 0, n_lat_blocks, s - 1)

    def tok(col_blk):
        return pl.BlockSpec((nb, LRU_BLOCK, GROUP_W), lambda s: (0, blk_of(s), col_blk))

    def const(shape):
        return pl.BlockSpec(shape, lambda s: (0,) * len(shape))

    gate_w = [const((GROUP_W, 2 * GROUP_W)), const((1, 2 * GROUP_W)), const((1, GROUP_W))]
    scratch = [pltpu.VMEM((nb, 1, GROUP_W), F32), pltpu.VMEM((nb, LRU_BLOCK, GROUP_W), F32),
               pltpu.VMEM((nb, LRU_BLOCK, GROUP_W), F32)]
    out_tok = pl.BlockSpec((nb, LRU_BLOCK, GROUP_W), lambda s: (0, blk_of(s), 0))
    out_sds = jax.ShapeDtypeStruct((nb, t, GROUP_W), F32)
    if rev:
        in_specs = [out_tok, tok(0), out_tok] + gate_w
        args = [xc, lru, hf, w_dir, bias_dir, lam_dir]
        out_specs, out_shape = out_tok, out_sds
    else:
        prev, nxt = _halo_specs(nb, LRU_BLOCK, GROUP_W, 1, blk_of, t)
        in_specs = [tok(1), prev, nxt, const((CONV_W, GROUP_W)), const((1, GROUP_W))] + gate_w
        args = [lru, lru, lru, cw, cb, w_dir, bias_dir, lam_dir]
        out_specs, out_shape = [out_tok, out_tok], [out_sds, out_sds]
        scratch.append(pltpu.VMEM((LRU_BLOCK + 2 * HALO, GROUP_W), F32))
    return pl.pallas_call(
        functools.partial(_lru_kernel, rev, n_lat_blocks),
        grid=(n_blocks,),
        in_specs=in_specs,
        out_specs=out_specs,
        out_shape=out_shape,
        scratch_shapes=scratch,
        compiler_params=_cparams(("arbitrary",)),
        name="lru_bwd" if rev else "lru_fwd",
    )(*args)


def _seg_mean(x2, seg_ref):
    return _dot_exact_rhs(x2, seg_ref[...])


def _rope(x, cos, sin_lo, sin_hi, half):
    w = x.shape[-1]
    rep = w // cos.shape[-1]
    cos, sin_lo, sin_hi = (jnp.concatenate([t] * rep, axis=1) if rep > 1 else t for t in (cos, sin_lo, sin_hi))
    return x * cos + pltpu.roll(x, w - half, 1) * sin_lo + pltpu.roll(x, half, 1) * sin_hi


def _qkv_prep(gqa, diff, gtab_ref, dtab_ref, seg64_ref, seg32_ref, gq_g_ref, gk_g_ref,
              dq_g_ref, dk_g_ref, gq_ref, gk_ref, gv_ref, dq_ref, dk_ref, dv_ref):
    def norm_rope(x, seg_ref, g, tab_ref, half, scale):
        w = x.shape[-1]
        xn = x * lax.rsqrt(_seg_mean(x * x, seg_ref)[:, :w] + NORM_EPS) * g[:, :w]
        xr = _rope(xn, tab_ref[0], tab_ref[1], tab_ref[2], half)
        return xr * scale if scale != 1.0 else xr

    def scatter(ref, x, n, d):
        for hh in range(n):
            ref[0, hh] = x[:, hh * d:(hh + 1) * d].astype(ref.dtype)

    def scatter_v(ref, x, n, d):
        ones = jnp.ones((x.shape[0], V_PAD - d), F32)
        for hh in range(n):
            ref[0, hh] = jnp.concatenate([x[:, hh * d:(hh + 1) * d], ones], axis=1).astype(ref.dtype)

    def scatter_t(ref, x, n, d):
        xt = x.T
        for hh in range(n):
            ref[0, hh, 0] = xt[hh * d:(hh + 1) * d, :].astype(ref.dtype)

    kw = GQA_KV * GQA_D
    q = norm_rope(gqa[:, :GROUP_W], seg64_ref, gq_g_ref[...], gtab_ref, GQA_D // 4, GQA_D ** -0.5 * LOG2E)
    k = _pad_lanes_rope(gqa[:, GROUP_W:GROUP_W + kw], seg64_ref, gk_g_ref[...], gtab_ref, GQA_D // 4)
    scatter(gq_ref, q, GQA_HEADS, GQA_D)
    scatter_t(gk_ref, k, GQA_KV, GQA_D)
    scatter_v(gv_ref, gqa[:, GROUP_W + kw:], GQA_KV, GQA_D)
    dq = norm_rope(diff[:, :GROUP_W], seg32_ref, dq_g_ref[...], dtab_ref, DIFF_QK // 4, DIFF_QK ** -0.5 * LOG2E)
    dk = norm_rope(diff[:, GROUP_W:2 * GROUP_W], seg32_ref, dk_g_ref[...], dtab_ref, DIFF_QK // 4, 1.0)
    scatter(dq_ref, dq, 2 * DIFF_HEADS, DIFF_QK)
    scatter_t(dk_ref, dk, 2 * DIFF_HEADS, DIFF_QK)
    scatter_v(dv_ref, diff[:, 2 * GROUP_W:], DIFF_HEADS, DIFF_V)


def _pad_lanes_rope(x, seg_ref, g, tab_ref, half):
    w = x.shape[-1]
    xn = x * lax.rsqrt(_dot_exact_rhs(x * x, seg_ref[:w, :w]) + NORM_EPS) * g[:, :w]
    return _rope(xn, tab_ref[0], tab_ref[1], tab_ref[2], half)


def _attn_kernel(diff_mode, post_scale, unroll, q_ref, kt_ref, v_ref, *rest):
    if diff_mode:
        lam_ref, sg_ref, o_ref, s_scr, m_scr, acc_scr, out_scr = rest
    else:
        o_ref, s_scr, m_scr, acc_scr, out_scr = rest
    tq = q_ref.shape[2]
    n_tiles = kt_ref.shape[2]
    n_units = v_ref.shape[1]
    dv = o_ref.shape[-1] // (n_units if diff_mode else 2 * n_units)

    def over_tiles(ctx_only, fn):
        if ctx_only:
            fn([n_tiles - 1])
        else:
            def body(i, carry):
                fn([i * unroll + j for j in range(unroll)])
                return carry
            lax.fori_loop(0, n_tiles // unroll, body, 0)

    def run_unit(u, ctx_only):
        if diff_mode:
            blocks = [(j * tq, q_ref[0, 2 * u + j], 2 * u + j) for j in range(2)]
        else:
            blocks = [(0, q_ref[0, pl.ds(2 * u, 2)].reshape(2 * tq, q_ref.shape[-1]), u)]
        m_scr[...] = jnp.full_like(m_scr, -jnp.inf)

        def scores(kts):
            for kt in kts:
                for r0, q, kh in blocks:
                    r = q.shape[0]
                    s = _dot(q, kt_ref[0, kh, kt])
                    s_scr[kt, r0:r0 + r, :] = s
                    m_scr[r0:r0 + r, :] = jnp.maximum(m_scr[r0:r0 + r, :], jnp.maximum(s[:, :128], s[:, 128:]))

        over_tiles(ctx_only, scores)
        m_scr[...] = jnp.broadcast_to(jnp.max(m_scr[...], axis=-1, keepdims=True), m_scr.shape)
        acc_scr[...] = jnp.zeros_like(acc_scr)

        def accumulate(kts):
            acc = None
            for kt in kts:
                m_rep = m_scr[...]
                p = jnp.exp2(s_scr[kt] - jnp.concatenate([m_rep, m_rep], axis=1))
                off = kt * KEY_TILE if isinstance(kt, int) else pl.multiple_of(kt * KEY_TILE, KEY_TILE)
                d = _dot(p.astype(BF16), v_ref[0, u, pl.ds(off, KEY_TILE), :])
                acc = d if acc is None else acc + d
            acc_scr[...] += acc

        over_tiles(ctx_only, accumulate)
        acc = acc_scr[...]
        return acc[:, :dv] / acc[:, dv:dv + 1]

    def finish(ctx_only):
        def unit(u, carry):
            o = run_unit(u, ctx_only)
            if diff_mode:
                o = o[:tq] - lam_ref[:, :o.shape[-1]] * o[tq:]
                out_scr[u] = _rms(o, sg_ref[...]) * post_scale
            else:
                out_scr[u] = jnp.concatenate([o[:tq], o[tq:]], axis=1)
            return carry

        lax.fori_loop(0, n_units, unit, 0)
        o_ref[0] = jnp.concatenate([out_scr[u] for u in range(n_units)], axis=1)

    is_ctx = pl.program_id(1) == pl.num_programs(1) - 1

    @pl.when(is_ctx)
    def _():
        finish(True)

    @pl.when(jnp.logical_not(is_ctx))
    def _():
        finish(False)


def _attention(diff_mode, q, kt, v, lam_row, subln_g, post_scale):
    nb, _, t, dqk = q.shape
    dv = GROUP_W // (GQA_HEADS if not diff_mode else DIFF_HEADS)
    n_tiles = kt.shape[2]
    n_units = v.shape[1]
    assert kt.shape[-1] == KEY_TILE
    unroll = max(u for u in range(1, ATT_MAX_UNROLL + 1) if n_tiles % u == 0)

    def resident(shape):
        return pl.BlockSpec(shape, lambda b, i: (b,) + (0,) * (len(shape) - 1), pipeline_mode=pl.Buffered(1))

    in_specs = [pl.BlockSpec((1, q.shape[1], ATT_TQ, dqk), lambda b, i: (b, 0, i, 0)),
                resident((1, kt.shape[1], n_tiles, dqk, KEY_TILE)),
                resident((1, n_units, t, V_PAD))]
    args = [q, kt, v]
    if diff_mode:
        in_specs += [pl.BlockSpec((1, 128), lambda b, i: (0, 0)), pl.BlockSpec((1, dv), lambda b, i: (0, 0))]
        args += [lam_row, subln_g]
    return pl.pallas_call(
        functools.partial(_attn_kernel, diff_mode, post_scale, unroll),
        grid=(nb, t // ATT_TQ),
        in_specs=in_specs,
        out_specs=pl.BlockSpec((1, ATT_TQ, GROUP_W), lambda b, i: (b, i, 0)),
        out_shape=jax.ShapeDtypeStruct((nb, t, GROUP_W), F32),
        scratch_shapes=[pltpu.VMEM((n_tiles, 2 * ATT_TQ, KEY_TILE), F32), pltpu.VMEM((2 * ATT_TQ, 128), F32),
                        pltpu.VMEM((2 * ATT_TQ, V_PAD), F32), pltpu.VMEM((n_units, ATT_TQ, GROUP_W // n_units), F32)],
        compiler_params=_cparams(("arbitrary", "arbitrary")),
        name="diff_attn" if diff_mode else "gqa_attn",
    )(*args)


def _outmlp_kernel(h_ref, ya_ref, yb_ref, yc_ref, yd_ref, mod_ref, g_ref, wo_ref, w1_ref, w2_ref, o_ref):
    mix = jnp.concatenate([ya_ref[...], yb_ref[...], yc_ref[...], yd_ref[...]], axis=1).astype(BF16)
    h1 = h_ref[...] + mod_ref[0, 2:3, :] * _dot(mix, wo_ref[0])
    v = _rms(h1, g_ref[...]) * (1.0 + mod_ref[0, 4:5, :]) + mod_ref[0, 3:4, :]
    u = jnp.maximum(_dot(v.astype(BF16), w1_ref[0]), 0.0)
    o_ref[...] = h1 + mod_ref[0, 5:6, :] * _dot((u * u).astype(BF16), w2_ref[0])


def _out_mlp(h, ya, yb, yc, yd, mod3, g2, w_out, w1, w2, layer, tpb, latent_only):
    n = h.shape[0]
    lat = tpb - 1
    n_tiles = (n // ROW_TILE) // tpb * lat if latent_only else n // ROW_TILE

    def src(i):
        return (i // lat) * tpb + i % lat if latent_only else i

    def mod_row(i):
        return i // lat if latent_only else _mod_row(i, tpb)

    def tok(w):
        return pl.BlockSpec((ROW_TILE, w), lambda i: (src(i), 0))

    def weight(k, m):
        return pl.BlockSpec((1, k, m), lambda i: (layer, 0, 0), pipeline_mode=pl.Buffered(1))

    return pl.pallas_call(
        _outmlp_kernel,
        grid=(n_tiles,),
        in_specs=[tok(D_MODEL), tok(GROUP_W), tok(GROUP_W), tok(GROUP_W), tok(GROUP_W),
                  pl.BlockSpec((1, N_MOD, D_MODEL), lambda i: (mod_row(i), 0, 0)),
                  pl.BlockSpec((1, D_MODEL), lambda i: (0, 0)),
                  weight(D_MODEL, D_MODEL), weight(D_MODEL, D_FF), weight(D_FF, D_MODEL)],
        out_specs=pl.BlockSpec((ROW_TILE, D_MODEL), lambda i: (i, 0)),
        out_shape=jax.ShapeDtypeStruct((n_tiles * ROW_TILE, D_MODEL), F32),
        compiler_params=_cparams(("arbitrary",)),
        name="out_mlp",
    )(h, ya, yb, yc, yd, mod3, g2, w_out, w1, w2)


def _rope_tables(s_len, head_dim, lanes=128):
    m = head_dim // 2
    half = m // 2
    lane = np.arange(lanes)
    d = lane % head_dim
    freq = ROPE_THETA ** (-(d % half).astype(np.float64) / half)
    t = np.arange(s_len)
    pos = np.where((d < m)[None, :], (t // GRID_W)[:, None], (t % GRID_W)[:, None]).astype(np.float32)
    ang = pos * freq.astype(np.float32)[None, :]
    cos, sin = np.cos(ang), np.sin(ang)
    low = ((d % m) < half)[None, :]
    tab = np.stack([cos, np.where(low, -sin, 0.0), np.where(low, 0.0, sin)])
    ident = np.stack([np.ones((CTX, lanes)), np.zeros((CTX, lanes)), np.zeros((CTX, lanes))])
    return jnp.asarray(np.concatenate([tab, ident], axis=1), F32)


def _seg_matrix(width, seg):
    idx = np.arange(width) // seg
    return jnp.asarray((idx[:, None] == idx[None, :]).astype(np.float32) / seg, BF16)


def _lane_tile(v, width):
    return jnp.tile(v, width // v.shape[-1]).reshape(1, width)


def _block_diag(w):
    nblk, j, k = w.shape
    eye = jnp.eye(nblk, dtype=w.dtype)
    return jnp.einsum('njk,nm->njmk', w, eye).reshape(nblk * j, nblk * k)


def kernel(x, c, ctx, c_ctx, w_mod, b_mod, norm1_g, w_in, ssd_conv_w, ssd_conv_b, ssd_a_log, ssd_dt_bias, ssd_d, ssd_norm_g, gqa_q_norm_g, gqa_k_norm_g, lru_conv_w, lru_conv_b, lru_w_r, lru_b_r, lru_w_i, lru_b_i, lru_lambda, diff_q_norm_g, diff_k_norm_g, diff_lambda_q1, diff_lambda_k1, diff_lambda_q2, diff_lambda_k2, diff_subln_g, w_out, norm2_g, w_mlp1, w_mlp2):
    nb, s_len, d_model = x.shape
    depth = w_mod.shape[0]
    assert d_model == D_MODEL and ctx.shape[1] == CTX and s_len % ROW_TILE == 0 and s_len % GRID_W == 0
    t = s_len + CTX
    tpb = t // ROW_TILE
    n = nb * t

    zx, dt_end = GROUP_W + SSD_XBC, GROUP_W + SSD_XBC + 2 * SSD_HEADS
    gqa_end, lru_end = dt_end + GQA_COLS, dt_end + GQA_COLS + LRU_COLS
    w_in_p = jnp.concatenate(
        [w_in[:, :, dt_end:gqa_end], w_in[:, :, lru_end:], w_in[:, :, :zx], w_in[:, :, gqa_end:lru_end],
         w_in[:, :, zx:dt_end], jnp.zeros((depth, D_MODEL, DT_PAD - 2 * SSD_HEADS), w_in.dtype)],
        axis=-1).astype(BF16)
    w_out_b, w1_b, w2_b = w_out.astype(BF16), w_mlp1.astype(BF16), w_mlp2.astype(BF16)
    lru_w = jnp.stack([jnp.concatenate([_block_diag(lru_w_r[l, d]), _block_diag(lru_w_i[l, d])], axis=1)
                       for l in range(depth) for d in range(2)]).reshape(depth, 2, GROUP_W, 2 * GROUP_W).astype(BF16)
    lru_b = jnp.concatenate([lru_b_r, lru_b_i], axis=-1)
    pad8 = DT_PAD - 2 * SSD_HEADS
    dtb_rows = jnp.pad(ssd_dt_bias.reshape(depth, 1, 2 * SSD_HEADS), ((0, 0), (0, 0), (0, pad8)))
    a_rows = jnp.pad(-jnp.exp(ssd_a_log.reshape(depth, 1, 2 * SSD_HEADS)), ((0, 0), (0, 0), (0, pad8)))
    d_rows = jnp.repeat(ssd_d, SSD_P, axis=-1)
    lam_diff = (jnp.exp(jnp.sum(diff_lambda_q1 * diff_lambda_k1, axis=-1))
                - jnp.exp(jnp.sum(diff_lambda_q2 * diff_lambda_k2, axis=-1)))
    gtab = _rope_tables(s_len, GQA_D)
    dtab = _rope_tables(s_len, DIFF_QK)
    seg64 = _seg_matrix(GROUP_W, GQA_D)
    seg32 = _seg_matrix(GROUP_W, DIFF_QK)

    c8 = jnp.concatenate([c, c_ctx[None, :], jnp.zeros((8 - nb - 1, D_MODEL), F32)], axis=0)
    mod = _modulation(c8, w_mod, b_mod)
    h = jnp.concatenate([x, ctx], axis=1).reshape(n, D_MODEL)

    for l in range(depth):
        lam_init = 0.8 - 0.6 * math.exp(-0.3 * l)
        mod3 = mod[l].reshape(8, N_MOD, D_MODEL)
        z, xbc, lru, dt, gq, gk, gv, dq, dk, dv = _in_proj(
            h, mod3, norm1_g[l].reshape(1, -1), w_in_p, l, gtab, dtab, seg64, seg32,
            _lane_tile(gqa_q_norm_g[l], GROUP_W), _lane_tile(gqa_k_norm_g[l], GROUP_W),
            _lane_tile(diff_q_norm_g[l], GROUP_W), _lane_tile(diff_k_norm_g[l], GROUP_W), nb, t)
        r3 = lambda a: a.reshape(nb, t, a.shape[-1])
        ssd_args = (r3(xbc), r3(dt), r3(z))
        cw, cb = ssd_conv_w[l], ssd_conv_b[l].reshape(1, -1)
        ya_f, xact = _ssd_direction(False, *ssd_args, None, cw, cb, dtb_rows[l], a_rows[l], d_rows[l, 0:1], None, t)
        ya = _ssd_direction(True, xact, *ssd_args[1:], ya_f, None, None, dtb_rows[l], a_rows[l], d_rows[l, 1:2],
                            ssd_norm_g[l].reshape(1, -1), t)
        lcw, lcb = lru_conv_w[l], lru_conv_b[l].reshape(1, -1)
        hf, xc = _lru_direction(False, r3(lru), None, None, lcw, lcb, lru_w[l, 0], lru_b[l, 0:1],
                                lru_lambda[l, 0:1], t)
        yc = _lru_direction(True, r3(lru), xc, hf, None, None, lru_w[l, 1], lru_b[l, 1:2], lru_lambda[l, 1:2], t)
        last = l == depth - 1
        yb = _attention(False, gq, gk, gv, None, None, 1.0)
        lam_row = jnp.full((1, 128), lam_init, F32) + lam_diff[l]
        yd = _attention(True, dq, dk, dv, lam_row, diff_subln_g[l].reshape(1, -1), 1.0 - lam_init)
        h = _out_mlp(h, ya.reshape(n, -1), yb.reshape(n, -1), yc.reshape(n, -1), yd.reshape(n, -1), mod3,
                     norm2_g[l].reshape(1, -1), w_out_b, w1_b, w2_b, l, tpb, last)
    return h.reshape(nb, s_len, D_MODEL)
```

```python
import functools
import math

import jax
import jax.numpy as jnp
import numpy as np
from jax import lax
from jax.experimental import pallas as pl
from jax.experimental.pallas import tpu as pltpu

F32 = jnp.float32
BF16 = jnp.bfloat16

D_MODEL = 1024
CTX = 256
GROUP_W = 256
D_FF = 4 * D_MODEL
N_MOD = 6
NORM_EPS = 1e-6
ROPE_THETA = 10000.0
GRID_W = 64
CONV_W = 4
CONV_PAD_L = CONV_W // 2
HALO = 8

SSD_HEADS = 4
SSD_P = 64
SSD_GROUPS = 2
SSD_N = 128
SSD_CHUNK = 128
SSD_BLOCK = 256
SSD_XBC = GROUP_W + 2 * SSD_GROUPS * SSD_N
GQA_HEADS = 4
GQA_KV = 2
GQA_D = 64
GQA_COLS = GROUP_W + 2 * GQA_KV * GQA_D
LRU_C = 8.0
LRU_COLS = 2 * GROUP_W
DIFF_HEADS = 4
DIFF_V = 64
DIFF_QK = 32
DIFF_COLS = 3 * GROUP_W
DT_PAD = 128
ATT_COLS = GQA_COLS + DIFF_COLS
IN_COLS_P = ATT_COLS + GROUP_W + SSD_XBC + LRU_COLS + DT_PAD

ROW_TILE = 256
LRU_BLOCK = 256
ATT_TQ = 256
KEY_TILE = 256
ATT_MAX_UNROLL = 33
V_PAD = 128
LOG2E = math.log2(math.e)
VMEM_LIMIT = 56 * 1024 * 1024


def _cparams(sem):
    return pltpu.CompilerParams(dimension_semantics=sem, vmem_limit_bytes=VMEM_LIMIT)


def _sigmoid(x):
    return 1.0 / (1.0 + jnp.exp(-x))


def _silu(x):
    return x * _sigmoid(x)


def _softplus(x):
    return jnp.maximum(x, 0.0) + jnp.log(1.0 + jnp.exp(-jnp.abs(x)))


def _rms(x, g):
    ms = jnp.mean(x * x, axis=-1, keepdims=True)
    return x * lax.rsqrt(ms + NORM_EPS) * g


def _dot(a, b):
    return jnp.dot(a, b, preferred_element_type=F32)


def _dot_nt(a, b):
    return lax.dot_general(a, b, (((1,), (1,)), ((), ())), preferred_element_type=F32)


def _dot_tn(a, b):
    return lax.dot_general(a, b, (((0,), (0,)), ((), ())), preferred_element_type=F32)


def _bf16_terms(x, n):
    terms = []
    for _ in range(n):
        t = x.astype(BF16)
        terms.append(t)
        x = x - t.astype(F32)
    return terms


def _dot_exact_lhs(a, x, n=3):
    return sum(_dot(a, t) for t in _bf16_terms(x, n))


def _dot_exact_rhs(x, b, n=2):
    return sum(_dot(t, b) for t in _bf16_terms(x, n))


def _mod_kernel(c_ref, w_ref, b_ref, o_ref):
    act = _silu(c_ref[...]).astype(BF16)
    o_ref[0] = _dot(act, w_ref[0].astype(BF16)) + b_ref[0]


def _modulation(c8, w_mod, b_mod):
    depth = w_mod.shape[0]
    tn = 1536
    return pl.pallas_call(
        _mod_kernel,
        grid=(depth, (N_MOD * D_MODEL) // tn),
        in_specs=[pl.BlockSpec((8, D_MODEL), lambda l, j: (0, 0)),
                  pl.BlockSpec((1, D_MODEL, tn), lambda l, j: (l, 0, j)),
                  pl.BlockSpec((1, 1, tn), lambda l, j: (l, 0, j))],
        out_specs=pl.BlockSpec((1, 8, tn), lambda l, j: (l, 0, j)),
        out_shape=jax.ShapeDtypeStruct((depth, 8, N_MOD * D_MODEL), F32),
        compiler_params=_cparams(("arbitrary", "arbitrary")),
        name="modulation",
    )(c8, w_mod, b_mod.reshape(depth, 1, -1))


def _mod_row(i, tpb):
    b = i // tpb
    return jnp.where(i % tpb == tpb - 1, 2, b)


def _inproj_kernel(h_ref, mod_ref, g_ref, w_ref, gtab_ref, dtab_ref, seg64_ref, seg32_ref, gq_g_ref, gk_g_ref,
                   dq_g_ref, dk_g_ref, z_ref, xbc_ref, lru_ref, dt_ref, gq_ref, gk_ref, gv_ref, dq_ref, dk_ref,
                   dv_ref):
    x = h_ref[...]
    u = (_rms(x, g_ref[...]) * (1.0 + mod_ref[0, 1:2, :]) + mod_ref[0, 0:1, :]).astype(BF16)
    att = _dot(u, w_ref[0, :, :ATT_COLS])
    _qkv_prep(att[:, :GQA_COLS], att[:, GQA_COLS:], gtab_ref, dtab_ref, seg64_ref, seg32_ref, gq_g_ref, gk_g_ref,
              dq_g_ref, dk_g_ref, gq_ref, gk_ref, gv_ref, dq_ref, dk_ref, dv_ref)
    p = _dot(u, w_ref[0, :, ATT_COLS:])
    c0 = 0
    for ref in (z_ref, xbc_ref, lru_ref, dt_ref):
        w = ref.shape[-1]
        ref[...] = p[:, c0:c0 + w]
        c0 += w


def _in_proj(h, mod3, g, w_in_p, layer, gtab, dtab, seg64, seg32, gq_g, gk_g, dq_g, dk_g, nb, t):
    n = h.shape[0]
    tpb = t // ROW_TILE
    widths = (GROUP_W, SSD_XBC, LRU_COLS, DT_PAD)

    def heads(nh, d):
        return (pl.BlockSpec((1, nh, ROW_TILE, d), lambda i: (i // tpb, 0, i % tpb, 0)),
                jax.ShapeDtypeStruct((nb, nh, t, d), BF16))

    def heads_t(nh, d):
        return (pl.BlockSpec((1, nh, 1, d, ROW_TILE), lambda i: (i // tpb, 0, i % tpb, 0, 0)),
                jax.ShapeDtypeStruct((nb, nh, tpb, d, ROW_TILE), BF16))

    outs = [(pl.BlockSpec((ROW_TILE, w), lambda i: (i, 0)), jax.ShapeDtypeStruct((n, w), F32)) for w in widths]
    outs += [heads(GQA_HEADS, GQA_D), heads_t(GQA_KV, GQA_D), heads(GQA_KV, V_PAD),
             heads(2 * DIFF_HEADS, DIFF_QK), heads_t(2 * DIFF_HEADS, DIFF_QK), heads(DIFF_HEADS, V_PAD)]

    def const(shape):
        return pl.BlockSpec(shape, lambda i: (0,) * len(shape))

    return pl.pallas_call(
        _inproj_kernel,
        grid=(n // ROW_TILE,),
        in_specs=[pl.BlockSpec((ROW_TILE, D_MODEL), lambda i: (i, 0)),
                  pl.BlockSpec((1, N_MOD, D_MODEL), lambda i: (_mod_row(i, tpb), 0, 0)),
                  pl.BlockSpec((1, D_MODEL), lambda i: (0, 0)),
                  pl.BlockSpec((1, D_MODEL, IN_COLS_P), lambda i: (layer, 0, 0)),
                  pl.BlockSpec((3, ROW_TILE, 128), lambda i: (0, i % tpb, 0)),
                  pl.BlockSpec((3, ROW_TILE, 128), lambda i: (0, i % tpb, 0)),
                  const((GROUP_W, GROUP_W)), const((GROUP_W, GROUP_W)),
                  const((1, GROUP_W)), const((1, GROUP_W)), const((1, GROUP_W)), const((1, GROUP_W))],
        out_specs=[o[0] for o in outs],
        out_shape=[o[1] for o in outs],
        compiler_params=_cparams(("arbitrary",)),
        name="in_proj",
    )(h, mod3, g, w_in_p, gtab, dtab, seg64, seg32, gq_g, gk_g, dq_g, dk_g)


def _conv_block(ext_ref, x, xp, xn, w_ref, b_ref, rows):
    ext_ref[0:HALO, :] = xp
    ext_ref[HALO:HALO + rows, :] = x
    ext_ref[HALO + rows:2 * HALO + rows, :] = xn
    acc = b_ref[...] + w_ref[0:1, :] * ext_ref[pl.ds(HALO - CONV_PAD_L, rows), :]
    for j in range(1, CONV_W):
        acc = acc + w_ref[j:j + 1, :] * ext_ref[pl.ds(HALO - CONV_PAD_L + j, rows), :]
    return acc


def _halo_specs(nb, rows, width, col_blk, blk_of_step, t):
    per = rows // HALO
    last = t // HALO - 1
    prev = pl.BlockSpec((nb, HALO, width), lambda s: (0, jnp.maximum(blk_of_step(s) * per - 1, 0), col_blk))
    nxt = pl.BlockSpec((nb, HALO, width), lambda s: (0, jnp.minimum((blk_of_step(s) + 1) * per, last), col_blk))
    return prev, nxt


def _ssd_kernel(rev, n_lat_blocks, *refs):
    if rev:
        xact_ref, dt_ref, z_ref, yf_ref, dtb_ref, a_ref, d_ref, ng_ref, o_ref, h_scr = refs
    else:
        (xbc_ref, xp_ref, xn_ref, dt_ref, cw_ref, cb_ref, dtb_ref, a_ref, d_ref,
         o_ref, xact_ref, h_scr, ext_scr) = refs
    nb = dt_ref.shape[0]
    q = SSD_CHUNK
    n_sub = SSD_BLOCK // q
    sub_order = range(n_sub - 1, -1, -1) if rev else range(n_sub)
    s = pl.program_id(0)
    n_steps = pl.num_programs(0)
    if rev:
        blk = n_steps - 1 - s
    else:
        blk = jnp.where(s == 0, n_lat_blocks, s - 1)
    seg_start = (blk == 0) | (blk == n_lat_blocks)
    seg_end = (blk == n_lat_blocks - 1) | (blk == n_steps - 1)

    @pl.when(s == 0)
    def _():
        h_scr[...] = jnp.zeros_like(h_scr)

    li = lax.broadcasted_iota(jnp.int32, (q, q), 0)
    si = lax.broadcasted_iota(jnp.int32, (q, q), 1)
    keep = (li <= si) if rev else (li >= si)
    tri = keep.astype(BF16)
    last = 0 if rev else q - 1
    col0 = SSD_HEADS if rev else 0

    pairs = [(b, k) for k in sub_order for b in range(nb)]
    pre = {}
    for b in range(nb):
        if rev:
            xbc_blk = xact_ref[b]
        else:
            xp = jnp.where(seg_start, 0.0, xp_ref[b])
            xn = jnp.where(seg_end, 0.0, xn_ref[b])
            xbc_blk = _silu(_conv_block(ext_scr, xbc_ref[b], xp, xn, cw_ref, cb_ref, SSD_BLOCK))
            xact_ref[b] = xbc_blk
        for k in range(n_sub):
            sp = _softplus(dt_ref[b, k * q:(k + 1) * q, :] + dtb_ref[...])
            acum = _dot_exact_lhs(tri, sp * a_ref[...])
            pre[b, k] = (xbc_blk[k * q:(k + 1) * q], sp, acum, acum.T)
    ys = {pair: [] for pair in pairs}
    for g in range(SSD_GROUPS):
        grp = {}
        for pair in pairs:
            xbc = pre[pair][0]
            bm = xbc[:, GROUP_W + g * SSD_N:GROUP_W + (g + 1) * SSD_N]
            cm = xbc[:, GROUP_W + (SSD_GROUPS + g) * SSD_N:GROUP_W + (SSD_GROUPS + g + 1) * SSD_N]
            grp[pair] = (bm.T.astype(BF16), cm, _dot_nt(cm.astype(BF16), bm.astype(BF16)))
        for hh in range(g * (SSD_HEADS // SSD_GROUPS), (g + 1) * (SSD_HEADS // SSD_GROUPS)):
            c = col0 + hh
            for pair in pairs:
                b = pair[0]
                xbc, sp, acum, acum_t = pre[pair]
                bm_t, cm, cb = grp[pair]
                ac = acum[:, c:c + 1]
                ar = acum_t[c:c + 1, :]
                tot = acum[last:last + 1, c:c + 1]
                decay = jnp.exp(jnp.where(keep, ac - ar, -jnp.inf))
                xdt = xbc[:, hh * SSD_P:(hh + 1) * SSD_P] * sp[:, c:c + 1]
                y = _dot((cb * decay).astype(BF16), xdt.astype(BF16))
                hin = h_scr[b, hh]
                y = y + _dot((cm * jnp.exp(ac)).astype(BF16), hin.astype(BF16))
                h_scr[b, hh] = hin * jnp.exp(tot) + _dot(bm_t, (xdt * jnp.exp(tot - ac)).astype(BF16))
                ys[pair].append(y)
    for b, k in pairs:
        rows = slice(k * q, (k + 1) * q)
        y = jnp.concatenate(ys[b, k], axis=1) + d_ref[...] * pre[b, k][0][:, 0:GROUP_W]
        if rev:
            y = (y + yf_ref[b, rows, :]) * _silu(z_ref[b, rows, :])
            half = GROUP_W // SSD_GROUPS
            y = jnp.concatenate([_rms(y[:, :half], ng_ref[:, :half]), _rms(y[:, half:], ng_ref[:, half:])], axis=1)
        o_ref[b, rows, :] = y


def _ssd_direction(rev, xbc, dt, z, yf, cw, cb, dtb, a_row, d_row, ng, t):
    nb = xbc.shape[0]
    assert CTX == SSD_BLOCK
    n_blocks = t // SSD_BLOCK
    n_lat_blocks = (t - CTX) // SSD_BLOCK

    def blk_of(s):
        if rev:
            return n_blocks - 1 - s
        return jnp.where(s == 0, n_lat_blocks, s - 1)

    def tok(width):
        return pl.BlockSpec((nb, SSD_BLOCK, width), lambda s: (0, blk_of(s), 0))

    def const(shape):
        return pl.BlockSpec(shape, lambda s: (0,) * len(shape))

    state = pltpu.VMEM((nb, SSD_HEADS, SSD_N, SSD_P), F32)
    small = [const((1, DT_PAD)), const((1, DT_PAD)), const((1, GROUP_W))]
    if rev:
        in_specs = [tok(SSD_XBC), tok(DT_PAD), tok(GROUP_W), tok(GROUP_W)] + small + [const((1, GROUP_W))]
        args = [xbc, dt, z, yf, dtb, a_row, d_row, ng]
        out_specs, out_shape, scratch = tok(GROUP_W), jax.ShapeDtypeStruct((nb, t, GROUP_W), F32), [state]
    else:
        prev, nxt = _halo_specs(nb, SSD_BLOCK, SSD_XBC, 0, blk_of, t)
        in_specs = [tok(SSD_XBC), prev, nxt, tok(DT_PAD), const((CONV_W, SSD_XBC)), const((1, SSD_XBC))] + small
        args = [xbc, xbc, xbc, dt, cw, cb, dtb, a_row, d_row]
        out_specs = [tok(GROUP_W), tok(SSD_XBC)]
        out_shape = [jax.ShapeDtypeStruct((nb, t, GROUP_W), F32), jax.ShapeDtypeStruct((nb, t, SSD_XBC), F32)]
        scratch = [state, pltpu.VMEM((SSD_BLOCK + 2 * HALO, SSD_XBC), F32)]
    return pl.pallas_call(
        functools.partial(_ssd_kernel, rev, n_lat_blocks),
        grid=(n_blocks,),
        in_specs=in_specs,
        out_specs=out_specs,
        out_shape=out_shape,
        scratch_shapes=scratch,
        compiler_params=_cparams(("arbitrary",)),
        name="ssd_bwd" if rev else "ssd_fwd",
    )(*args)


def _lru_kernel(rev, n_lat_blocks, *refs):
    if rev:
        xc_ref, gate_ref, hf_ref, w_ref, bias_ref, lam_ref, o_ref, h_scr, a_scr, u_scr = refs
    else:
        (x_ref, xp_ref, xn_ref, cw_ref, cb_ref, w_ref, bias_ref, lam_ref,
         o_ref, xc_ref, h_scr, a_scr, u_scr, ext_scr) = refs
    nb = o_ref.shape[0]
    rows = LRU_BLOCK
    s = pl.program_id(0)
    n_steps = pl.num_programs(0)
    if rev:
        blk = n_steps - 1 - s
    else:
        blk = jnp.where(s == 0, n_lat_blocks, s - 1)
    seg_start = (blk == 0) | (blk == n_lat_blocks)
    seg_end = (blk == n_lat_blocks - 1) | (blk == n_lat_blocks)

    @pl.when(s == 0)
    def _():
        h_scr[...] = jnp.zeros_like(h_scr)

    sp_lam = _softplus(-lam_ref[...])
    for b in range(nb):
        if rev:
            xc = xc_ref[b]
        else:
            xp = jnp.where(seg_start, 0.0, xp_ref[b])
            xn = jnp.where(seg_end, 0.0, xn_ref[b])
            xc = _conv_block(ext_scr, x_ref[b], xp, xn, cw_ref, cb_ref, rows)
            xc_ref[b] = xc
        ri = _dot(xc.astype(BF16), w_ref[...]) + bias_ref[...]
        r = _sigmoid(ri[:, :GROUP_W])
        gi = _sigmoid(ri[:, GROUP_W:])
        log_a = -LRU_C * r * sp_lam
        a_scr[b] = jnp.exp(log_a)
        u_scr[b] = jnp.sqrt(1.0 - jnp.exp(2.0 * log_a)) * (gi * xc)

    row_id = lax.broadcasted_iota(jnp.int32, (8, GROUP_W), 0)

    def tile_scan(a, u):
        for d in (1, 2, 4):
            shift, valid = (8 - d, row_id < 8 - d) if rev else (d, row_id >= d)
            u = u + a * jnp.where(valid, pltpu.roll(u, shift, 0), 0.0)
            a = a * jnp.where(valid, pltpu.roll(a, shift, 0), 1.0)
        return a, u

    def step(g8, hs):
        hs = list(hs)
        r0 = pl.multiple_of(((rows // 8 - 1 - g8) if rev else g8) * 8, 8)
        for b in range(nb):
            a, u = tile_scan(a_scr[b, pl.ds(r0, 8), :], u_scr[b, pl.ds(r0, 8), :])
            h = u + a * hs[b]
            u_scr[b, pl.ds(r0, 8), :] = h
            hs[b] = h[0:1] if rev else h[7:8]
        return tuple(hs)

    hs = lax.fori_loop(0, rows // 8, step, tuple(h_scr[b] for b in range(nb)))
    for b in range(nb):
        h_scr[b] = hs[b]
        if rev:
            gt = gate_ref[b]
            gelu = 0.5 * gt * (1.0 + jnp.tanh(math.sqrt(2.0 / math.pi) * (gt + 0.044715 * gt * gt * gt)))
            o_ref[b] = gelu * (hf_ref[b] + u_scr[b])
        else:
            o_ref[b] = u_scr[b]


def _lru_direction(rev, lru, xc, hf, cw, cb, w_dir, bias_dir, lam_dir, t):
    nb = lru.shape[0]
    n_blocks = t // LRU_BLOCK
    n_lat_blocks = (t - CTX) // LRU_BLOCK

    def blk_of(s):
        if rev:
            return n_blocks - 1 - s
        return jnp.where(s == 0, n_lat_blocks, s - 1)

    def tok(col_blk):
        return pl.BlockSpec((nb, LRU_BLOCK, GROUP_W), lambda s: (0, blk_of(s), col_blk))

    def const(shape):
        return pl.BlockSpec(shape, lambda s: (0,) * len(shape))

    gate_w = [const((GROUP_W, 2 * GROUP_W)), const((1, 2 * GROUP_W)), const((1, GROUP_W))]
    scratch = [pltpu.VMEM((nb, 1, GROUP_W), F32), pltpu.VMEM((nb, LRU_BLOCK, GROUP_W), F32),
               pltpu.VMEM((nb, LRU_BLOCK, GROUP_W), F32)]
    out_tok = pl.BlockSpec((nb, LRU_BLOCK, GROUP_W), lambda s: (0, blk_of(s), 0))
    out_sds = jax.ShapeDtypeStruct((nb, t, GROUP_W), F32)
    if rev:
        in_specs = [out_tok, tok(0), out_tok] + gate_w
        args = [xc, lru, hf, w_dir, bias_dir, lam_dir]
        out_specs, out_shape = out_tok, out_sds
    else:
        prev, nxt = _halo_specs(nb, LRU_BLOCK, GROUP_W, 1, blk_of, t)
        in_specs = [tok(1), prev, nxt, const((CONV_W, GROUP_W)), const((1, GROUP_W))] + gate_w
        args = [lru, lru, lru, cw, cb, w_dir, bias_dir, lam_dir]
        out_specs, out_shape = [out_tok, out_tok], [out_sds, out_sds]
        scratch.append(pltpu.VMEM((LRU_BLOCK + 2 * HALO, GROUP_W), F32))
    return pl.pallas_call(
        functools.partial(_lru_kernel, rev, n_lat_blocks),
        grid=(n_blocks,),
        in_specs=in_specs,
        out_specs=out_specs,
        out_shape=out_shape,
        scratch_shapes=scratch,
        compiler_params=_cparams(("arbitrary",)),
        name="lru_bwd" if rev else "lru_fwd",
    )(*args)


def _seg_mean(x2, seg_ref):
    return _dot_exact_rhs(x2, seg_ref[...])


def _rope(x, cos, sin_lo, sin_hi, half):
    w = x.shape[-1]
    rep = w // cos.shape[-1]
    cos, sin_lo, sin_hi = (jnp.concatenate([t] * rep, axis=1) if rep > 1 else t for t in (cos, sin_lo, sin_hi))
    return x * cos + pltpu.roll(x, w - half, 1) * sin_lo + pltpu.roll(x, half, 1) * sin_hi


def _qkv_prep(gqa, diff, gtab_ref, dtab_ref, seg64_ref, seg32_ref, gq_g_ref, gk_g_ref,
              dq_g_ref, dk_g_ref, gq_ref, gk_ref, gv_ref, dq_ref, dk_ref, dv_ref):
    def norm_rope(x, seg_ref, g, tab_ref, half, scale):
        w = x.shape[-1]
        xn = x * lax.rsqrt(_seg_mean(x * x, seg_ref)[:, :w] + NORM_EPS) * g[:, :w]
        xr = _rope(xn, tab_ref[0], tab_ref[1], tab_ref[2], half)
        return xr * scale if scale != 1.0 else xr

    def scatter(ref, x, n, d):
        for hh in range(n):
            ref[0, hh] = x[:, hh * d:(hh + 1) * d].astype(ref.dtype)

    def scatter_v(ref, x, n, d):
        ones = jnp.ones((x.shape[0], V_PAD - d), F32)
        for hh in range(n):
            ref[0, hh] = jnp.concatenate([x[:, hh * d:(hh + 1) * d], ones], axis=1).astype(ref.dtype)

    def scatter_t(ref, x, n, d):
        xt = x.T
        for hh in range(n):
            ref[0, hh, 0] = xt[hh * d:(hh + 1) * d, :].astype(ref.dtype)

    kw = GQA_KV * GQA_D
    q = norm_rope(gqa[:, :GROUP_W], seg64_ref, gq_g_ref[...], gtab_ref, GQA_D // 4, GQA_D ** -0.5 * LOG2E)
    k = _pad_lanes_rope(gqa[:, GROUP_W:GROUP_W + kw], seg64_ref, gk_g_ref[...], gtab_ref, GQA_D // 4)
    scatter(gq_ref, q, GQA_HEADS, GQA_D)
    scatter_t(gk_ref, k, GQA_KV, GQA_D)
    scatter_v(gv_ref, gqa[:, GROUP_W + kw:], GQA_KV, GQA_D)
    dq = norm_rope(diff[:, :GROUP_W], seg32_ref, dq_g_ref[...], dtab_ref, DIFF_QK // 4, DIFF_QK ** -0.5 * LOG2E)
    dk = norm_rope(diff[:, GROUP_W:2 * GROUP_W], seg32_ref, dk_g_ref[...], dtab_ref, DIFF_QK // 4, 1.0)
    scatter(dq_ref, dq, 2 * DIFF_HEADS, DIFF_QK)
    scatter_t(dk_ref, dk, 2 * DIFF_HEADS, DIFF_QK)
    scatter_v(dv_ref, diff[:, 2 * GROUP_W:], DIFF_HEADS, DIFF_V)


def _pad_lanes_rope(x, seg_ref, g, tab_ref, half):
    w = x.shape[-1]
    xn = x * lax.rsqrt(_dot_exact_rhs(x * x, seg_ref[:w, :w]) + NORM_EPS) * g[:, :w]
    return _rope(xn, tab_ref[0], tab_ref[1], tab_ref[2], half)


def _attn_kernel(diff_mode, post_scale, unroll, q_ref, kt_ref, v_ref, *rest):
    if diff_mode:
        lam_ref, sg_ref, o_ref, s_scr, m_scr, acc_scr = rest
    else:
        o_ref, s_scr, m_scr, acc_scr = rest
    tq = q_ref.shape[2]
    n_tiles = kt_ref.shape[2]
    n_units = v_ref.shape[1]
    dv = o_ref.shape[-1] // 2

    def over_tiles(ctx_only, fn):
        if ctx_only:
            fn([n_tiles - 1])
        else:
            def body(i, carry):
                fn([i * unroll + j for j in range(unroll)])
                return carry
            lax.fori_loop(0, n_tiles // unroll, body, 0)

    def run_unit(u, ctx_only):
        if diff_mode:
            blocks = [(j * tq, q_ref[0, 2 * u + j], 2 * u + j) for j in range(2)]
        else:
            blocks = [(0, q_ref[0].reshape(2 * tq, q_ref.shape[-1]), u)]
        m_scr[...] = jnp.full_like(m_scr, -jnp.inf)

        def scores(kts):
            for kt in kts:
                for r0, q, kh in blocks:
                    r = q.shape[0]
                    s = _dot(q, kt_ref[0, kh, kt])
                    s_scr[kt, r0:r0 + r, :] = s
                    m_scr[r0:r0 + r, :] = jnp.maximum(m_scr[r0:r0 + r, :], jnp.maximum(s[:, :128], s[:, 128:]))

        over_tiles(ctx_only, scores)
        m_scr[...] = jnp.broadcast_to(jnp.max(m_scr[...], axis=-1, keepdims=True), m_scr.shape)
        acc_scr[...] = jnp.zeros_like(acc_scr)

        def accumulate(kts):
            acc = None
            for kt in kts:
                m_rep = m_scr[...]
                p = jnp.exp2(s_scr[kt] - jnp.concatenate([m_rep, m_rep], axis=1))
                off = kt * KEY_TILE if isinstance(kt, int) else pl.multiple_of(kt * KEY_TILE, KEY_TILE)
                d = _dot(p.astype(BF16), v_ref[0, u, pl.ds(off, KEY_TILE), :])
                acc = d if acc is None else acc + d
            acc_scr[...] += acc

        over_tiles(ctx_only, accumulate)
        acc = acc_scr[...]
        return acc[:, :dv] / acc[:, dv:dv + 1]

    def finish(ctx_only):
        outs = []
        for u in range(n_units):
            o = run_unit(u, ctx_only)
            if diff_mode:
                o = o[:tq] - lam_ref[:, :o.shape[-1]] * o[tq:]
                outs.append(_rms(o, sg_ref[...]) * post_scale)
            else:
                outs += [o[:tq], o[tq:]]
        o_ref[0] = jnp.concatenate(outs, axis=1)

    is_ctx = pl.program_id(2) == pl.num_programs(2) - 1

    @pl.when(is_ctx)
    def _():
        finish(True)

    @pl.when(jnp.logical_not(is_ctx))
    def _():
        finish(False)


def _attention(diff_mode, q, kt, v, lam_row, subln_g, post_scale):
    nb, _, t, dqk = q.shape
    dv = GROUP_W // (GQA_HEADS if not diff_mode else DIFF_HEADS)
    n_tiles = kt.shape[2]
    n_units = 2 if diff_mode else 1
    n_groups = v.shape[1] // n_units
    qh = q.shape[1] // n_groups
    kh = kt.shape[1] // n_groups
    assert kt.shape[-1] == KEY_TILE
    unroll = max(u for u in range(1, ATT_MAX_UNROLL + 1) if n_tiles % u == 0)
    in_specs = [pl.BlockSpec((1, qh, ATT_TQ, dqk), lambda b, g, i: (b, g, i, 0)),
                pl.BlockSpec((1, kh, n_tiles, dqk, KEY_TILE), lambda b, g, i: (b, g, 0, 0, 0)),
                pl.BlockSpec((1, n_units, t, V_PAD), lambda b, g, i: (b, g, 0, 0))]
    args = [q, kt, v]
    if diff_mode:
        in_specs += [pl.BlockSpec((1, 128), lambda b, g, i: (0, 0)), pl.BlockSpec((1, dv), lambda b, g, i: (0, 0))]
        args += [lam_row, subln_g]
    return pl.pallas_call(
        functools.partial(_attn_kernel, diff_mode, post_scale, unroll),
        grid=(nb, n_groups, t // ATT_TQ),
        in_specs=in_specs,
        out_specs=pl.BlockSpec((1, ATT_TQ, 128), lambda b, g, i: (b, i, g)),
        out_shape=jax.ShapeDtypeStruct((nb, t, GROUP_W), F32),
        scratch_shapes=[pltpu.VMEM((n_tiles, 2 * ATT_TQ, KEY_TILE), F32), pltpu.VMEM((2 * ATT_TQ, 128), F32),
                        pltpu.VMEM((2 * ATT_TQ, V_PAD), F32)],
        compiler_params=_cparams(("arbitrary", "arbitrary", "arbitrary")),
        name="diff_attn" if diff_mode else "gqa_attn",
    )(*args)


def _outmlp_kernel(h_ref, ya_ref, yb_ref, yc_ref, yd_ref, mod_ref, g_ref, wo_ref, w1_ref, w2_ref, o_ref):
    mix = jnp.concatenate([ya_ref[...], yb_ref[...], yc_ref[...], yd_ref[...]], axis=1).astype(BF16)
    h1 = h_ref[...] + mod_ref[0, 2:3, :] * _dot(mix, wo_ref[0])
    v = _rms(h1, g_ref[...]) * (1.0 + mod_ref[0, 4:5, :]) + mod_ref[0, 3:4, :]
    u = jnp.maximum(_dot(v.astype(BF16), w1_ref[0]), 0.0)
    o_ref[...] = h1 + mod_ref[0, 5:6, :] * _dot((u * u).astype(BF16), w2_ref[0])


def _out_mlp(h, ya, yb, yc, yd, mod3, g2, w_out, w1, w2, layer, tpb, latent_only):
    n = h.shape[0]
    lat = tpb - 1
    n_tiles = (n // ROW_TILE) // tpb * lat if latent_only else n // ROW_TILE

    def src(i):
        return (i // lat) * tpb + i % lat if latent_only else i

    def mod_row(i):
        return i // lat if latent_only else _mod_row(i, tpb)

    def tok(w):
        return pl.BlockSpec((ROW_TILE, w), lambda i: (src(i), 0))

    def weight(k, m):
        return pl.BlockSpec((1, k, m), lambda i: (layer, 0, 0), pipeline_mode=pl.Buffered(1))

    return pl.pallas_call(
        _outmlp_kernel,
        grid=(n_tiles,),
        in_specs=[tok(D_MODEL), tok(GROUP_W), tok(GROUP_W), tok(GROUP_W), tok(GROUP_W),
                  pl.BlockSpec((1, N_MOD, D_MODEL), lambda i: (mod_row(i), 0, 0)),
                  pl.BlockSpec((1, D_MODEL), lambda i: (0, 0)),
                  weight(D_MODEL, D_MODEL), weight(D_MODEL, D_FF), weight(D_FF, D_MODEL)],
        out_specs=pl.BlockSpec((ROW_TILE, D_MODEL), lambda i: (i, 0)),
        out_shape=jax.ShapeDtypeStruct((n_tiles * ROW_TILE, D_MODEL), F32),
        compiler_params=_cparams(("arbitrary",)),
        name="out_mlp",
    )(h, ya, yb, yc, yd, mod3, g2, w_out, w1, w2)


def _rope_tables(s_len, head_dim, lanes=128):
    m = head_dim // 2
    half = m // 2
    lane = np.arange(lanes)
    d = lane % head_dim
    freq = ROPE_THETA ** (-(d % half).astype(np.float64) / half)
    t = np.arange(s_len)
    pos = np.where((d < m)[None, :], (t // GRID_W)[:, None], (t % GRID_W)[:, None]).astype(np.float32)
    ang = pos * freq.astype(np.float32)[None, :]
    cos, sin = np.cos(ang), np.sin(ang)
    low = ((d % m) < half)[None, :]
    tab = np.stack([cos, np.where(low, -sin, 0.0), np.where(low, 0.0, sin)])
    ident = np.stack([np.ones((CTX, lanes)), np.zeros((CTX, lanes)), np.zeros((CTX, lanes))])
    return jnp.asarray(np.concatenate([tab, ident], axis=1), F32)


def _seg_matrix(width, seg):
    idx = np.arange(width) // seg
    return jnp.asarray((idx[:, None] == idx[None, :]).astype(np.float32) / seg, BF16)


def _lane_tile(v, width):
    return jnp.tile(v, width // v.shape[-1]).reshape(1, width)


def _block_diag(w):
    nblk, j, k = w.shape
    eye = jnp.eye(nblk, dtype=w.dtype)
    return jnp.einsum('njk,nm->njmk', w, eye).reshape(nblk * j, nblk * k)


def kernel(x, c, ctx, c_ctx, w_mod, b_mod, norm1_g, w_in, ssd_conv_w, ssd_conv_b, ssd_a_log, ssd_dt_bias, ssd_d, ssd_norm_g, gqa_q_norm_g, gqa_k_norm_g, lru_conv_w, lru_conv_b, lru_w_r, lru_b_r, lru_w_i, lru_b_i, lru_lambda, diff_q_norm_g, diff_k_norm_g, diff_lambda_q1, diff_lambda_k1, diff_lambda_q2, diff_lambda_k2, diff_subln_g, w_out, norm2_g, w_mlp1, w_mlp2):
    nb, s_len, d_model = x.shape
    depth = w_mod.shape[0]
    assert d_model == D_MODEL and ctx.shape[1] == CTX and s_len % ROW_TILE == 0 and s_len % GRID_W == 0
    t = s_len + CTX
    tpb = t // ROW_TILE
    n = nb * t

    zx, dt_end = GROUP_W + SSD_XBC, GROUP_W + SSD_XBC + 2 * SSD_HEADS
    gqa_end, lru_end = dt_end + GQA_COLS, dt_end + GQA_COLS + LRU_COLS
    w_in_p = jnp.concatenate(
        [w_in[:, :, dt_end:gqa_end], w_in[:, :, lru_end:], w_in[:, :, :zx], w_in[:, :, gqa_end:lru_end],
         w_in[:, :, zx:dt_end], jnp.zeros((depth, D_MODEL, DT_PAD - 2 * SSD_HEADS), w_in.dtype)],
        axis=-1).astype(BF16)
    w_out_b, w1_b, w2_b = w_out.astype(BF16), w_mlp1.astype(BF16), w_mlp2.astype(BF16)
    lru_w = jnp.stack([jnp.concatenate([_block_diag(lru_w_r[l, d]), _block_diag(lru_w_i[l, d])], axis=1)
                       for l in range(depth) for d in range(2)]).reshape(depth, 2, GROUP_W, 2 * GROUP_W).astype(BF16)
    lru_b = jnp.concatenate([lru_b_r, lru_b_i], axis=-1)
    pad8 = DT_PAD - 2 * SSD_HEADS
    dtb_rows = jnp.pad(ssd_dt_bias.reshape(depth, 1, 2 * SSD_HEADS), ((0, 0), (0, 0), (0, pad8)))
    a_rows = jnp.pad(-jnp.exp(ssd_a_log.reshape(depth, 1, 2 * SSD_HEADS)), ((0, 0), (0, 0), (0, pad8)))
    d_rows = jnp.repeat(ssd_d, SSD_P, axis=-1)
    lam_diff = (jnp.exp(jnp.sum(diff_lambda_q1 * diff_lambda_k1, axis=-1))
                - jnp.exp(jnp.sum(diff_lambda_q2 * diff_lambda_k2, axis=-1)))
    gtab = _rope_tables(s_len, GQA_D)
    dtab = _rope_tables(s_len, DIFF_QK)
    seg64 = _seg_matrix(GROUP_W, GQA_D)
    seg32 = _seg_matrix(GROUP_W, DIFF_QK)

    c8 = jnp.concatenate([c, c_ctx[None, :], jnp.zeros((8 - nb - 1, D_MODEL), F32)], axis=0)
    mod = _modulation(c8, w_mod, b_mod)
    h = jnp.concatenate([x, ctx], axis=1).reshape(n, D_MODEL)

    for l in range(depth):
        lam_init = 0.8 - 0.6 * math.exp(-0.3 * l)
        mod3 = mod[l].reshape(8, N_MOD, D_MODEL)
        z, xbc, lru, dt, gq, gk, gv, dq, dk, dv = _in_proj(
            h, mod3, norm1_g[l].reshape(1, -1), w_in_p, l, gtab, dtab, seg64, seg32,
            _lane_tile(gqa_q_norm_g[l], GROUP_W), _lane_tile(gqa_k_norm_g[l], GROUP_W),
            _lane_tile(diff_q_norm_g[l], GROUP_W), _lane_tile(diff_k_norm_g[l], GROUP_W), nb, t)
        r3 = lambda a: a.reshape(nb, t, a.shape[-1])
        ssd_args = (r3(xbc), r3(dt), r3(z))
        cw, cb = ssd_conv_w[l], ssd_conv_b[l].reshape(1, -1)
        ya_f, xact = _ssd_direction(False, *ssd_args, None, cw, cb, dtb_rows[l], a_rows[l], d_rows[l, 0:1], None, t)
        ya = _ssd_direction(True, xact, *ssd_args[1:], ya_f, None, None, dtb_rows[l], a_rows[l], d_rows[l, 1:2],
                            ssd_norm_g[l].reshape(1, -1), t)
        lcw, lcb = lru_conv_w[l], lru_conv_b[l].reshape(1, -1)
        hf, xc = _lru_direction(False, r3(lru), None, None, lcw, lcb, lru_w[l, 0], lru_b[l, 0:1],
                                lru_lambda[l, 0:1], t)
        yc = _lru_direction(True, r3(lru), xc, hf, None, None, lru_w[l, 1], lru_b[l, 1:2], lru_lambda[l, 1:2], t)
        last = l == depth - 1
        yb = _attention(False, gq, gk, gv, None, None, 1.0)
        lam_row = jnp.full((1, 128), lam_init, F32) + lam_diff[l]
        yd = _attention(True, dq, dk, dv, lam_row, diff_subln_g[l].reshape(1, -1), 1.0 - lam_init)
        h = _out_mlp(h, ya.reshape(n, -1), yb.reshape(n, -1), yc.reshape(n, -1), yd.reshape(n, -1), mod3,
                     norm2_g[l].reshape(1, -1), w_out_b, w1_b, w2_b, l, tpb, last)
    return h.reshape(nb, s_len, D_MODEL)
```

```python
import functools
import math

import jax
import jax.numpy as jnp
import numpy as np
from jax import lax
from jax.experimental import pallas as pl
from jax.experimental.pallas import tpu as pltpu

F32 = jnp.float32
BF16 = jnp.bfloat16

D_MODEL = 1024
CTX = 256
GROUP_W = 256
D_FF = 4 * D_MODEL
N_MOD = 6
NORM_EPS = 1e-6
ROPE_THETA = 10000.0
GRID_W = 64
CONV_W = 4
CONV_PAD_L = CONV_W // 2
HALO = 8

SSD_HEADS = 4
SSD_P = 64
SSD_GROUPS = 2
SSD_N = 128
SSD_CHUNK = 128
SSD_BLOCK = 256
SSD_XBC = GROUP_W + 2 * SSD_GROUPS * SSD_N
GQA_HEADS = 4
GQA_KV = 2
GQA_D = 64
GQA_COLS = GROUP_W + 2 * GQA_KV * GQA_D
LRU_C = 8.0
LRU_COLS = 2 * GROUP_W
DIFF_HEADS = 4
DIFF_V = 64
DIFF_QK = 32
DIFF_COLS = 3 * GROUP_W
DT_PAD = 128
ATT_COLS = GQA_COLS + DIFF_COLS
IN_COLS_P = ATT_COLS + GROUP_W + SSD_XBC + LRU_COLS + DT_PAD

ROW_TILE = 256
LRU_BLOCK = 256
ATT_TQ = 256
KEY_TILE = 256
ATT_MAX_UNROLL = 33
V_PAD = 128
LOG2E = math.log2(math.e)
VMEM_LIMIT = 56 * 1024 * 1024


def _cparams(sem):
    return pltpu.CompilerParams(dimension_semantics=sem, vmem_limit_bytes=VMEM_LIMIT)


def _sigmoid(x):
    return 1.0 / (1.0 + jnp.exp(-x))


def _silu(x):
    return x * _sigmoid(x)


def _softplus(x):
    return jnp.maximum(x, 0.0) + jnp.log(1.0 + jnp.exp(-jnp.abs(x)))


def _rms(x, g):
    ms = jnp.mean(x * x, axis=-1, keepdims=True)
    return x * lax.rsqrt(ms + NORM_EPS) * g


def _dot(a, b):
    return jnp.dot(a, b, preferred_element_type=F32)


def _dot_nt(a, b):
    return lax.dot_general(a, b, (((1,), (1,)), ((), ())), preferred_element_type=F32)


def _dot_tn(a, b):
    return lax.dot_general(a, b, (((0,), (0,)), ((), ())), preferred_element_type=F32)


def _bf16_terms(x, n):
    terms = []
    for _ in range(n):
        t = x.astype(BF16)
        terms.append(t)
        x = x - t.astype(F32)
    return terms


def _dot_exact_lhs(a, x, n=3):
    return sum(_dot(a, t) for t in _bf16_terms(x, n))


def _dot_exact_rhs(x, b, n=2):
    return sum(_dot(t, b) for t in _bf16_terms(x, n))


def _mod_kernel(c_ref, w_ref, b_ref, o_ref):
    act = _silu(c_ref[...]).astype(BF16)
    o_ref[0] = _dot(act, w_ref[0].astype(BF16)) + b_ref[0]


def _modulation(c8, w_mod, b_mod):
    depth = w_mod.shape[0]
    tn = 1536
    return pl.pallas_call(
        _mod_kernel,
        grid=(depth, (N_MOD * D_MODEL) // tn),
        in_specs=[pl.BlockSpec((8, D_MODEL), lambda l, j: (0, 0)),
                  pl.BlockSpec((1, D_MODEL, tn), lambda l, j: (l, 0, j)),
                  pl.BlockSpec((1, 1, tn), lambda l, j: (l, 0, j))],
        out_specs=pl.BlockSpec((1, 8, tn), lambda l, j: (l, 0, j)),
        out_shape=jax.ShapeDtypeStruct((depth, 8, N_MOD * D_MODEL), F32),
        compiler_params=_cparams(("arbitrary", "arbitrary")),
        name="modulation",
    )(c8, w_mod, b_mod.reshape(depth, 1, -1))


def _mod_row(i, tpb):
    b = i // tpb
    return jnp.where(i % tpb == tpb - 1, 2, b)


def _inproj_kernel(h_ref, mod_ref, g_ref, w_ref, gtab_ref, dtab_ref, seg64_ref, seg32_ref, gq_g_ref, gk_g_ref,
                   dq_g_ref, dk_g_ref, z_ref, xbc_ref, lru_ref, dt_ref, gq_ref, gk_ref, gv_ref, dq_ref, dk_ref,
                   dv_ref):
    x = h_ref[...]
    u = (_rms(x, g_ref[...]) * (1.0 + mod_ref[0, 1:2, :]) + mod_ref[0, 0:1, :]).astype(BF16)
    att = _dot(u, w_ref[0, :, :ATT_COLS])
    _qkv_prep(att[:, :GQA_COLS], att[:, GQA_COLS:], gtab_ref, dtab_ref, seg64_ref, seg32_ref, gq_g_ref, gk_g_ref,
              dq_g_ref, dk_g_ref, gq_ref, gk_ref, gv_ref, dq_ref, dk_ref, dv_ref)
    p = _dot(u, w_ref[0, :, ATT_COLS:])
    c0 = 0
    for ref in (z_ref, xbc_ref, lru_ref, dt_ref):
        w = ref.shape[-1]
        ref[...] = p[:, c0:c0 + w]
        c0 += w


def _in_proj(h, mod3, g, w_in_p, layer, gtab, dtab, seg64, seg32, gq_g, gk_g, dq_g, dk_g, nb, t):
    n = h.shape[0]
    tpb = t // ROW_TILE
    widths = (GROUP_W, SSD_XBC, LRU_COLS, DT_PAD)

    def heads(nh, d):
        return (pl.BlockSpec((1, nh, ROW_TILE, d), lambda i: (i // tpb, 0, i % tpb, 0)),
                jax.ShapeDtypeStruct((nb, nh, t, d), BF16))

    def heads_t(nh, d):
        return (pl.BlockSpec((1, nh, 1, d, ROW_TILE), lambda i: (i // tpb, 0, i % tpb, 0, 0)),
                jax.ShapeDtypeStruct((nb, nh, tpb, d, ROW_TILE), BF16))

    outs = [(pl.BlockSpec((ROW_TILE, w), lambda i: (i, 0)), jax.ShapeDtypeStruct((n, w), F32)) for w in widths]
    outs += [heads(GQA_HEADS, GQA_D), heads_t(GQA_KV, GQA_D), heads(GQA_KV, V_PAD),
             heads(2 * DIFF_HEADS, DIFF_QK), heads_t(2 * DIFF_HEADS, DIFF_QK), heads(DIFF_HEADS, V_PAD)]

    def const(shape):
        return pl.BlockSpec(shape, lambda i: (0,) * len(shape))

    return pl.pallas_call(
        _inproj_kernel,
        grid=(n // ROW_TILE,),
        in_specs=[pl.BlockSpec((ROW_TILE, D_MODEL), lambda i: (i, 0)),
                  pl.BlockSpec((1, N_MOD, D_MODEL), lambda i: (_mod_row(i, tpb), 0, 0)),
                  pl.BlockSpec((1, D_MODEL), lambda i: (0, 0)),
                  pl.BlockSpec((1, D_MODEL, IN_COLS_P), lambda i: (layer, 0, 0)),
                  pl.BlockSpec((3, ROW_TILE, 128), lambda i: (0, i % tpb, 0)),
                  pl.BlockSpec((3, ROW_TILE, 128), lambda i: (0, i % tpb, 0)),
                  const((GROUP_W, GROUP_W)), const((GROUP_W, GROUP_W)),
                  const((1, GROUP_W)), const((1, GROUP_W)), const((1, GROUP_W)), const((1, GROUP_W))],
        out_specs=[o[0] for o in outs],
        out_shape=[o[1] for o in outs],
        compiler_params=_cparams(("arbitrary",)),
        name="in_proj",
    )(h, mod3, g, w_in_p, gtab, dtab, seg64, seg32, gq_g, gk_g, dq_g, dk_g)


def _conv_block(ext_ref, x, xp, xn, w_ref, b_ref, rows):
    ext_ref[0:HALO, :] = xp
    ext_ref[HALO:HALO + rows, :] = x
    ext_ref[HALO + rows:2 * HALO + rows, :] = xn
    acc = b_ref[...] + w_ref[0:1, :] * ext_ref[pl.ds(HALO - CONV_PAD_L, rows), :]
    for j in range(1, CONV_W):
        acc = acc + w_ref[j:j + 1, :] * ext_ref[pl.ds(HALO - CONV_PAD_L + j, rows), :]
    return acc


def _halo_specs(nb, rows, width, col_blk, blk_of_step, t):
    per = rows // HALO
    last = t // HALO - 1
    prev = pl.BlockSpec((nb, HALO, width), lambda s: (0, jnp.maximum(blk_of_step(s) * per - 1, 0), col_blk))
    nxt = pl.BlockSpec((nb, HALO, width), lambda s: (0, jnp.minimum((blk_of_step(s) + 1) * per, last), col_blk))
    return prev, nxt


def _ssd_step(rev, n_lat_blocks, *refs):
    if rev:
        xact_ref, dt_ref, z_ref, yf_ref, dtb_ref, a_ref, d_ref, ng_ref, o_ref, h_scr = refs
    else:
        (xbc_ref, xp_ref, xn_ref, dt_ref, cw_ref, cb_ref, dtb_ref, a_ref, d_ref,
         o_ref, xact_ref, h_scr, ext_scr) = refs
    nb = dt_ref.shape[0]
    q = SSD_CHUNK
    n_sub = SSD_BLOCK // q
    sub_order = range(n_sub - 1, -1, -1) if rev else range(n_sub)
    s = pl.program_id(0)
    n_steps = pl.num_programs(0)
    if rev:
        blk = n_steps - 1 - s
    else:
        blk = jnp.where(s == 0, n_lat_blocks, s - 1)
    seg_start = (blk == 0) | (blk == n_lat_blocks)
    seg_end = (blk == n_lat_blocks - 1) | (blk == n_steps - 1)

    li = lax.broadcasted_iota(jnp.int32, (q, q), 0)
    si = lax.broadcasted_iota(jnp.int32, (q, q), 1)
    keep = (li <= si) if rev else (li >= si)
    tri = keep.astype(BF16)
    last = 0 if rev else q - 1
    col0 = SSD_HEADS if rev else 0

    pairs = [(b, k) for k in sub_order for b in range(nb)]
    pre = {}
    for b in range(nb):
        if rev:
            xbc_blk = xact_ref[b]
        else:
            xp = jnp.where(seg_start, 0.0, xp_ref[b])
            xn = jnp.where(seg_end, 0.0, xn_ref[b])
            xbc_blk = _silu(_conv_block(ext_scr, xbc_ref[b], xp, xn, cw_ref, cb_ref, SSD_BLOCK))
            xact_ref[b] = xbc_blk
        for k in range(n_sub):
            sp = _softplus(dt_ref[b, k * q:(k + 1) * q, :] + dtb_ref[...])
            acum = _dot_exact_lhs(tri, sp * a_ref[...])
            pre[b, k] = (xbc_blk[k * q:(k + 1) * q], sp, acum, acum.T)
    ys = {pair: [] for pair in pairs}
    for g in range(SSD_GROUPS):
        grp = {}
        for pair in pairs:
            xbc = pre[pair][0]
            bm = xbc[:, GROUP_W + g * SSD_N:GROUP_W + (g + 1) * SSD_N]
            cm = xbc[:, GROUP_W + (SSD_GROUPS + g) * SSD_N:GROUP_W + (SSD_GROUPS + g + 1) * SSD_N]
            grp[pair] = (bm.T.astype(BF16), cm, _dot_nt(cm.astype(BF16), bm.astype(BF16)))
        for hh in range(g * (SSD_HEADS // SSD_GROUPS), (g + 1) * (SSD_HEADS // SSD_GROUPS)):
            c = col0 + hh
            for pair in pairs:
                b = pair[0]
                xbc, sp, acum, acum_t = pre[pair]
                bm_t, cm, cb = grp[pair]
                ac = acum[:, c:c + 1]
                ar = acum_t[c:c + 1, :]
                tot = acum[last:last + 1, c:c + 1]
                decay = jnp.exp(jnp.where(keep, ac - ar, -jnp.inf))
                xdt = xbc[:, hh * SSD_P:(hh + 1) * SSD_P] * sp[:, c:c + 1]
                y = _dot((cb * decay).astype(BF16), xdt.astype(BF16))
                hin = h_scr[b, hh]
                y = y + _dot((cm * jnp.exp(ac)).astype(BF16), hin.astype(BF16))
                h_scr[b, hh] = hin * jnp.exp(tot) + _dot(bm_t, (xdt * jnp.exp(tot - ac)).astype(BF16))
                ys[pair].append(y)
    for b, k in pairs:
        rows = slice(k * q, (k + 1) * q)
        y = jnp.concatenate(ys[b, k], axis=1) + d_ref[...] * pre[b, k][0][:, 0:GROUP_W]
        if rev:
            y = (y + yf_ref[b, rows, :]) * _silu(z_ref[b, rows, :])
            half = GROUP_W // SSD_GROUPS
            y = jnp.concatenate([_rms(y[:, :half], ng_ref[:, :half]), _rms(y[:, half:], ng_ref[:, half:])], axis=1)
        o_ref[b, rows, :] = y


def _ssd_plan(rev, blk_of, xbc, dt, z, yf, cw, cb, dtb, a_row, d_row, ng, t):
    nb = xbc.shape[0]

    def tok(width):
        return pl.BlockSpec((nb, SSD_BLOCK, width), lambda s: (0, blk_of(s), 0))

    def const(shape):
        return pl.BlockSpec(shape, lambda s: (0,) * len(shape))

    state = pltpu.VMEM((nb, SSD_HEADS, SSD_N, SSD_P), F32)
    small = [const((1, DT_PAD)), const((1, DT_PAD)), const((1, GROUP_W))]
    if rev:
        in_specs = [tok(SSD_XBC), tok(DT_PAD), tok(GROUP_W), tok(GROUP_W)] + small + [const((1, GROUP_W))]
        args = [xbc, dt, z, yf, dtb, a_row, d_row, ng]
        out_specs, out_shape, scratch = [tok(GROUP_W)], [jax.ShapeDtypeStruct((nb, t, GROUP_W), F32)], [state]
    else:
        prev, nxt = _halo_specs(nb, SSD_BLOCK, SSD_XBC, 0, blk_of, t)
        in_specs = [tok(SSD_XBC), prev, nxt, tok(DT_PAD), const((CONV_W, SSD_XBC)), const((1, SSD_XBC))] + small
        args = [xbc, xbc, xbc, dt, cw, cb, dtb, a_row, d_row]
        out_specs = [tok(GROUP_W), tok(SSD_XBC)]
        out_shape = [jax.ShapeDtypeStruct((nb, t, GROUP_W), F32), jax.ShapeDtypeStruct((nb, t, SSD_XBC), F32)]
        scratch = [state, pltpu.VMEM((SSD_BLOCK + 2 * HALO, SSD_XBC), F32)]
    return in_specs, args, out_specs, out_shape, scratch


def _lru_step(rev, n_lat_blocks, *refs):
    if rev:
        xc_ref, gate_ref, hf_ref, w_ref, bias_ref, lam_ref, o_ref, h_scr, a_scr, u_scr = refs
    else:
        (x_ref, xp_ref, xn_ref, cw_ref, cb_ref, w_ref, bias_ref, lam_ref,
         o_ref, xc_ref, h_scr, a_scr, u_scr, ext_scr) = refs
    nb = o_ref.shape[0]
    rows = LRU_BLOCK
    s = pl.program_id(0)
    n_steps = pl.num_programs(0)
    if rev:
        blk = n_steps - 1 - s
    else:
        blk = jnp.where(s == 0, n_lat_blocks, s - 1)
    seg_start = (blk == 0) | (blk == n_lat_blocks)
    seg_end = (blk == n_lat_blocks - 1) | (blk == n_lat_blocks)

    sp_lam = _softplus(-lam_ref[...])
    for b in range(nb):
        if rev:
            xc = xc_ref[b]
        else:
            xp = jnp.where(seg_start, 0.0, xp_ref[b])
            xn = jnp.where(seg_end, 0.0, xn_ref[b])
            xc = _conv_block(ext_scr, x_ref[b], xp, xn, cw_ref, cb_ref, rows)
            xc_ref[b] = xc
        ri = _dot(xc.astype(BF16), w_ref[...]) + bias_ref[...]
        r = _sigmoid(ri[:, :GROUP_W])
        gi = _sigmoid(ri[:, GROUP_W:])
        log_a = -LRU_C * r * sp_lam
        a_scr[b] = jnp.exp(log_a)
        u_scr[b] = jnp.sqrt(1.0 - jnp.exp(2.0 * log_a)) * (gi * xc)

    row_id = lax.broadcasted_iota(jnp.int32, (8, GROUP_W), 0)

    def tile_scan(a, u):
        for d in (1, 2, 4):
            shift, valid = (8 - d, row_id < 8 - d) if rev else (d, row_id >= d)
            u = u + a * jnp.where(valid, pltpu.roll(u, shift, 0), 0.0)
            a = a * jnp.where(valid, pltpu.roll(a, shift, 0), 1.0)
        return a, u

    hs = [h_scr[b] for b in range(nb)]
    for g8 in range(rows // 8):
        r0 = ((rows // 8 - 1 - g8) if rev else g8) * 8
        for b in range(nb):
            a, u = tile_scan(a_scr[b, r0:r0 + 8, :], u_scr[b, r0:r0 + 8, :])
            h = u + a * hs[b]
            u_scr[b, r0:r0 + 8, :] = h
            hs[b] = h[0:1] if rev else h[7:8]
    for b in range(nb):
        h_scr[b] = hs[b]
        if rev:
            gt = gate_ref[b]
            gelu = 0.5 * gt * (1.0 + jnp.tanh(math.sqrt(2.0 / math.pi) * (gt + 0.044715 * gt * gt * gt)))
            o_ref[b] = gelu * (hf_ref[b] + u_scr[b])
        else:
            o_ref[b] = u_scr[b]


def _lru_plan(rev, blk_of, lru, xc, hf, cw, cb, w_dir, bias_dir, lam_dir, t):
    nb = lru.shape[0]

    def tok(col_blk):
        return pl.BlockSpec((nb, LRU_BLOCK, GROUP_W), lambda s: (0, blk_of(s), col_blk))

    def const(shape):
        return pl.BlockSpec(shape, lambda s: (0,) * len(shape))

    gate_w = [const((GROUP_W, 2 * GROUP_W)), const((1, 2 * GROUP_W)), const((1, GROUP_W))]
    scratch = [pltpu.VMEM((nb, 1, GROUP_W), F32), pltpu.VMEM((nb, LRU_BLOCK, GROUP_W), F32),
               pltpu.VMEM((nb, LRU_BLOCK, GROUP_W), F32)]
    out_tok = pl.BlockSpec((nb, LRU_BLOCK, GROUP_W), lambda s: (0, blk_of(s), 0))
    out_sds = jax.ShapeDtypeStruct((nb, t, GROUP_W), F32)
    if rev:
        in_specs = [out_tok, tok(0), out_tok] + gate_w
        args = [xc, lru, hf, w_dir, bias_dir, lam_dir]
        out_specs, out_shape = [out_tok], [out_sds]
    else:
        prev, nxt = _halo_specs(nb, LRU_BLOCK, GROUP_W, 1, blk_of, t)
        in_specs = [tok(1), prev, nxt, const((CONV_W, GROUP_W)), const((1, GROUP_W))] + gate_w
        args = [lru, lru, lru, cw, cb, w_dir, bias_dir, lam_dir]
        out_specs, out_shape = [out_tok, out_tok], [out_sds, out_sds]
        scratch.append(pltpu.VMEM((LRU_BLOCK + 2 * HALO, GROUP_W), F32))
    return in_specs, args, out_specs, out_shape, scratch


def _scan_kernel(rev, n_lat_blocks, counts, *refs):
    (n_in_s, n_in_l), (n_out_s, n_out_l), (n_scr_s, _) = counts
    n_in, n_out = n_in_s + n_in_l, n_out_s + n_out_l
    ins, outs, scr = refs[:n_in], refs[n_in:n_in + n_out], refs[n_in + n_out:]

    @pl.when(pl.program_id(0) == 0)
    def _():
        for state in (scr[0], scr[n_scr_s]):
            state[...] = jnp.zeros_like(state)

    _ssd_step(rev, n_lat_blocks, *ins[:n_in_s], *outs[:n_out_s], *scr[:n_scr_s])
    _lru_step(rev, n_lat_blocks, *ins[n_in_s:], *outs[n_out_s:], *scr[n_scr_s:])


def _scans(rev, ssd_operands, lru_operands, t):
    assert CTX == SSD_BLOCK == LRU_BLOCK
    n_blocks = t // SSD_BLOCK
    n_lat_blocks = (t - CTX) // SSD_BLOCK

    def blk_of(s):
        if rev:
            return n_blocks - 1 - s
        return jnp.where(s == 0, n_lat_blocks, s - 1)

    ssd = _ssd_plan(rev, blk_of, *ssd_operands, t)
    lru = _lru_plan(rev, blk_of, *lru_operands, t)
    counts = tuple((len(a), len(b)) for a, b in zip(ssd, lru))
    counts = (counts[0], counts[2], counts[4])
    return pl.pallas_call(
        functools.partial(_scan_kernel, rev, n_lat_blocks, counts),
        grid=(n_blocks,),
        in_specs=ssd[0] + lru[0],
        out_specs=ssd[2] + lru[2],
        out_shape=ssd[3] + lru[3],
        scratch_shapes=ssd[4] + lru[4],
        compiler_params=_cparams(("arbitrary",)),
        name="scan_bwd" if rev else "scan_fwd",
    )(*ssd[1], *lru[1])


def _seg_mean(x2, seg_ref):
    return _dot_exact_rhs(x2, seg_ref[...])


def _rope(x, cos, sin_lo, sin_hi, half):
    w = x.shape[-1]
    rep = w // cos.shape[-1]
    cos, sin_lo, sin_hi = (jnp.concatenate([t] * rep, axis=1) if rep > 1 else t for t in (cos, sin_lo, sin_hi))
    return x * cos + pltpu.roll(x, w - half, 1) * sin_lo + pltpu.roll(x, half, 1) * sin_hi


def _qkv_prep(gqa, diff, gtab_ref, dtab_ref, seg64_ref, seg32_ref, gq_g_ref, gk_g_ref,
              dq_g_ref, dk_g_ref, gq_ref, gk_ref, gv_ref, dq_ref, dk_ref, dv_ref):
    def norm_rope(x, seg_ref, g, tab_ref, half, scale):
        w = x.shape[-1]
        xn = x * lax.rsqrt(_seg_mean(x * x, seg_ref)[:, :w] + NORM_EPS) * g[:, :w]
        xr = _rope(xn, tab_ref[0], tab_ref[1], tab_ref[2], half)
        return xr * scale if scale != 1.0 else xr

    def scatter(ref, x, n, d):
        for hh in range(n):
            ref[0, hh] = x[:, hh * d:(hh + 1) * d].astype(ref.dtype)

    def scatter_v(ref, x, n, d):
        ones = jnp.ones((x.shape[0], V_PAD - d), F32)
        for hh in range(n):
            ref[0, hh] = jnp.concatenate([x[:, hh * d:(hh + 1) * d], ones], axis=1).astype(ref.dtype)

    def scatter_t(ref, x, n, d):
        xt = x.T
        for hh in range(n):
            ref[0, hh, 0] = xt[hh * d:(hh + 1) * d, :].astype(ref.dtype)

    kw = GQA_KV * GQA_D
    q = norm_rope(gqa[:, :GROUP_W], seg64_ref, gq_g_ref[...], gtab_ref, GQA_D // 4, GQA_D ** -0.5 * LOG2E)
    k = _pad_lanes_rope(gqa[:, GROUP_W:GROUP_W + kw], seg64_ref, gk_g_ref[...], gtab_ref, GQA_D // 4)
    scatter(gq_ref, q, GQA_HEADS, GQA_D)
    scatter_t(gk_ref, k, GQA_KV, GQA_D)
    scatter_v(gv_ref, gqa[:, GROUP_W + kw:], GQA_KV, GQA_D)
    dq = norm_rope(diff[:, :GROUP_W], seg32_ref, dq_g_ref[...], dtab_ref, DIFF_QK // 4, DIFF_QK ** -0.5 * LOG2E)
    dk = norm_rope(diff[:, GROUP_W:2 * GROUP_W], seg32_ref, dk_g_ref[...], dtab_ref, DIFF_QK // 4, 1.0)
    scatter(dq_ref, dq, 2 * DIFF_HEADS, DIFF_QK)
    scatter_t(dk_ref, dk, 2 * DIFF_HEADS, DIFF_QK)
    scatter_v(dv_ref, diff[:, 2 * GROUP_W:], DIFF_HEADS, DIFF_V)


def _pad_lanes_rope(x, seg_ref, g, tab_ref, half):
    w = x.shape[-1]
    xn = x * lax.rsqrt(_dot_exact_rhs(x * x, seg_ref[:w, :w]) + NORM_EPS) * g[:, :w]
    return _rope(xn, tab_ref[0], tab_ref[1], tab_ref[2], half)


def _attn_kernel(diff_mode, post_scale, unroll, q_ref, kt_ref, v_ref, *rest):
    if diff_mode:
        lam_ref, sg_ref, o_ref, s_scr, m_scr, acc_scr = rest
    else:
        o_ref, s_scr, m_scr, acc_scr = rest
    tq = q_ref.shape[2]
    n_tiles = kt_ref.shape[2]
    n_units = v_ref.shape[1]
    dv = o_ref.shape[-1] // 2

    def over_tiles(ctx_only, fn):
        if ctx_only:
            fn([n_tiles - 1])
        else:
            def body(i, carry):
                fn([i * unroll + j for j in range(unroll)])
                return carry
            lax.fori_loop(0, n_tiles // unroll, body, 0)

    def run_unit(u, ctx_only):
        if diff_mode:
            blocks = [(j * tq, q_ref[0, 2 * u + j], 2 * u + j) for j in range(2)]
        else:
            blocks = [(0, q_ref[0].reshape(2 * tq, q_ref.shape[-1]), u)]
        m_scr[...] = jnp.full_like(m_scr, -jnp.inf)

        def scores(kts):
            for kt in kts:
                for r0, q, kh in blocks:
                    r = q.shape[0]
                    s = _dot(q, kt_ref[0, kh, kt])
                    s_scr[kt, r0:r0 + r, :] = s
                    m_scr[r0:r0 + r, :] = jnp.maximum(m_scr[r0:r0 + r, :], jnp.maximum(s[:, :128], s[:, 128:]))

        over_tiles(ctx_only, scores)
        m_scr[...] = jnp.broadcast_to(jnp.max(m_scr[...], axis=-1, keepdims=True), m_scr.shape)
        acc_scr[...] = jnp.zeros_like(acc_scr)

        def accumulate(kts):
            acc = None
            for kt in kts:
                m_rep = m_scr[...]
                p = jnp.exp2(s_scr[kt] - jnp.concatenate([m_rep, m_rep], axis=1))
                off = kt * KEY_TILE if isinstance(kt, int) else pl.multiple_of(kt * KEY_TILE, KEY_TILE)
                d = _dot(p.astype(BF16), v_ref[0, u, pl.ds(off, KEY_TILE), :])
                acc = d if acc is None else acc + d
            acc_scr[...] += acc

        over_tiles(ctx_only, accumulate)
        acc = acc_scr[...]
        return acc[:, :dv] / acc[:, dv:dv + 1]

    def finish(ctx_only):
        outs = []
        for u in range(n_units):
            o = run_unit(u, ctx_only)
            if diff_mode:
                o = o[:tq] - lam_ref[:, :o.shape[-1]] * o[tq:]
                outs.append(_rms(o, sg_ref[...]) * post_scale)
            else:
                outs += [o[:tq], o[tq:]]
        o_ref[0] = jnp.concatenate(outs, axis=1)

    is_ctx = pl.program_id(2) == pl.num_programs(2) - 1

    @pl.when(is_ctx)
    def _():
        finish(True)

    @pl.when(jnp.logical_not(is_ctx))
    def _():
        finish(False)


def _attention(diff_mode, q, kt, v, lam_row, subln_g, post_scale):
    nb, _, t, dqk = q.shape
    dv = GROUP_W // (GQA_HEADS if not diff_mode else DIFF_HEADS)
    n_tiles = kt.shape[2]
    n_units = 2 if diff_mode else 1
    n_groups = v.shape[1] // n_units
    qh = q.shape[1] // n_groups
    kh = kt.shape[1] // n_groups
    assert kt.shape[-1] == KEY_TILE
    unroll = max(u for u in range(1, ATT_MAX_UNROLL + 1) if n_tiles % u == 0)
    in_specs = [pl.BlockSpec((1, qh, ATT_TQ, dqk), lambda b, g, i: (b, g, i, 0)),
                pl.BlockSpec((1, kh, n_tiles, dqk, KEY_TILE), lambda b, g, i: (b, g, 0, 0, 0)),
                pl.BlockSpec((1, n_units, t, V_PAD), lambda b, g, i: (b, g, 0, 0))]
    args = [q, kt, v]
    if diff_mode:
        in_specs += [pl.BlockSpec((1, 128), lambda b, g, i: (0, 0)), pl.BlockSpec((1, dv), lambda b, g, i: (0, 0))]
        args += [lam_row, subln_g]
    return pl.pallas_call(
        functools.partial(_attn_kernel, diff_mode, post_scale, unroll),
        grid=(nb, n_groups, t // ATT_TQ),
        in_specs=in_specs,
        out_specs=pl.BlockSpec((1, ATT_TQ, 128), lambda b, g, i: (b, i, g)),
        out_shape=jax.ShapeDtypeStruct((nb, t, GROUP_W), F32),
        scratch_shapes=[pltpu.VMEM((n_tiles, 2 * ATT_TQ, KEY_TILE), F32), pltpu.VMEM((2 * ATT_TQ, 128), F32),
                        pltpu.VMEM((2 * ATT_TQ, V_PAD), F32)],
        compiler_params=_cparams(("arbitrary", "arbitrary", "arbitrary")),
        name="diff_attn" if diff_mode else "gqa_attn",
    )(*args)


def _outmlp_kernel(h_ref, ya_ref, yb_ref, yc_ref, yd_ref, mod_ref, g_ref, wo_ref, w1_ref, w2_ref, o_ref):
    mix = jnp.concatenate([ya_ref[...], yb_ref[...], yc_ref[...], yd_ref[...]], axis=1).astype(BF16)
    h1 = h_ref[...] + mod_ref[0, 2:3, :] * _dot(mix, wo_ref[0])
    v = _rms(h1, g_ref[...]) * (1.0 + mod_ref[0, 4:5, :]) + mod_ref[0, 3:4, :]
    u = jnp.maximum(_dot(v.astype(BF16), w1_ref[0]), 0.0)
    o_ref[...] = h1 + mod_ref[0, 5:6, :] * _dot((u * u).astype(BF16), w2_ref[0])


def _out_mlp(h, ya, yb, yc, yd, mod3, g2, w_out, w1, w2, layer, tpb, latent_only):
    n = h.shape[0]
    lat = tpb - 1
    n_tiles = (n // ROW_TILE) // tpb * lat if latent_only else n // ROW_TILE

    def src(i):
        return (i // lat) * tpb + i % lat if latent_only else i

    def mod_row(i):
        return i // lat if latent_only else _mod_row(i, tpb)

    def tok(w):
        return pl.BlockSpec((ROW_TILE, w), lambda i: (src(i), 0))

    def weight(k, m):
        return pl.BlockSpec((1, k, m), lambda i: (layer, 0, 0), pipeline_mode=pl.Buffered(1))

    return pl.pallas_call(
        _outmlp_kernel,
        grid=(n_tiles,),
        in_specs=[tok(D_MODEL), tok(GROUP_W), tok(GROUP_W), tok(GROUP_W), tok(GROUP_W),
                  pl.BlockSpec((1, N_MOD, D_MODEL), lambda i: (mod_row(i), 0, 0)),
                  pl.BlockSpec((1, D_MODEL), lambda i: (0, 0)),
                  weight(D_MODEL, D_MODEL), weight(D_MODEL, D_FF), weight(D_FF, D_MODEL)],
        out_specs=pl.BlockSpec((ROW_TILE, D_MODEL), lambda i: (i, 0)),
        out_shape=jax.ShapeDtypeStruct((n_tiles * ROW_TILE, D_MODEL), F32),
        compiler_params=_cparams(("arbitrary",)),
        name="out_mlp",
    )(h, ya, yb, yc, yd, mod3, g2, w_out, w1, w2)


def _rope_tables(s_len, head_dim, lanes=128):
    m = head_dim // 2
    half = m // 2
    lane = np.arange(lanes)
    d = lane % head_dim
    freq = ROPE_THETA ** (-(d % half).astype(np.float64) / half)
    t = np.arange(s_len)
    pos = np.where((d < m)[None, :], (t // GRID_W)[:, None], (t % GRID_W)[:, None]).astype(np.float32)
    ang = pos * freq.astype(np.float32)[None, :]
    cos, sin = np.cos(ang), np.sin(ang)
    low = ((d % m) < half)[None, :]
    tab = np.stack([cos, np.where(low, -sin, 0.0), np.where(low, 0.0, sin)])
    ident = np.stack([np.ones((CTX, lanes)), np.zeros((CTX, lanes)), np.zeros((CTX, lanes))])
    return jnp.asarray(np.concatenate([tab, ident], axis=1), F32)


def _seg_matrix(width, seg):
    idx = np.arange(width) // seg
    return jnp.asarray((idx[:, None] == idx[None, :]).astype(np.float32) / seg, BF16)


def _lane_tile(v, width):
    return jnp.tile(v, width // v.shape[-1]).reshape(1, width)


def _block_diag(w):
    nblk, j, k = w.shape
    eye = jnp.eye(nblk, dtype=w.dtype)
    return jnp.einsum('njk,nm->njmk', w, eye).reshape(nblk * j, nblk * k)


def kernel(x, c, ctx, c_ctx, w_mod, b_mod, norm1_g, w_in, ssd_conv_w, ssd_conv_b, ssd_a_log, ssd_dt_bias, ssd_d, ssd_norm_g, gqa_q_norm_g, gqa_k_norm_g, lru_conv_w, lru_conv_b, lru_w_r, lru_b_r, lru_w_i, lru_b_i, lru_lambda, diff_q_norm_g, diff_k_norm_g, diff_lambda_q1, diff_lambda_k1, diff_lambda_q2, diff_lambda_k2, diff_subln_g, w_out, norm2_g, w_mlp1, w_mlp2):
    nb, s_len, d_model = x.shape
    depth = w_mod.shape[0]
    assert d_model == D_MODEL and ctx.shape[1] == CTX and s_len % ROW_TILE == 0 and s_len % GRID_W == 0
    t = s_len + CTX
    tpb = t // ROW_TILE
    n = nb * t

    zx, dt_end = GROUP_W + SSD_XBC, GROUP_W + SSD_XBC + 2 * SSD_HEADS
    gqa_end, lru_end = dt_end + GQA_COLS, dt_end + GQA_COLS + LRU_COLS
    w_in_p = jnp.concatenate(
        [w_in[:, :, dt_end:gqa_end], w_in[:, :, lru_end:], w_in[:, :, :zx], w_in[:, :, gqa_end:lru_end],
         w_in[:, :, zx:dt_end], jnp.zeros((depth, D_MODEL, DT_PAD - 2 * SSD_HEADS), w_in.dtype)],
        axis=-1).astype(BF16)
    w_out_b, w1_b, w2_b = w_out.astype(BF16), w_mlp1.astype(BF16), w_mlp2.astype(BF16)
    lru_w = jnp.stack([jnp.concatenate([_block_diag(lru_w_r[l, d]), _block_diag(lru_w_i[l, d])], axis=1)
                       for l in range(depth) for d in range(2)]).reshape(depth, 2, GROUP_W, 2 * GROUP_W).astype(BF16)
    lru_b = jnp.concatenate([lru_b_r, lru_b_i], axis=-1)
    pad8 = DT_PAD - 2 * SSD_HEADS
    dtb_rows = jnp.pad(ssd_dt_bias.reshape(depth, 1, 2 * SSD_HEADS), ((0, 0), (0, 0), (0, pad8)))
    a_rows = jnp.pad(-jnp.exp(ssd_a_log.reshape(depth, 1, 2 * SSD_HEADS)), ((0, 0), (0, 0), (0, pad8)))
    d_rows = jnp.repeat(ssd_d, SSD_P, axis=-1)
    lam_diff = (jnp.exp(jnp.sum(diff_lambda_q1 * diff_lambda_k1, axis=-1))
                - jnp.exp(jnp.sum(diff_lambda_q2 * diff_lambda_k2, axis=-1)))
    gtab = _rope_tables(s_len, GQA_D)
    dtab = _rope_tables(s_len, DIFF_QK)
    seg64 = _seg_matrix(GROUP_W, GQA_D)
    seg32 = _seg_matrix(GROUP_W, DIFF_QK)

    c8 = jnp.concatenate([c, c_ctx[None, :], jnp.zeros((8 - nb - 1, D_MODEL), F32)], axis=0)
    mod = _modulation(c8, w_mod, b_mod)
    h = jnp.concatenate([x, ctx], axis=1).reshape(n, D_MODEL)

    for l in range(depth):
        lam_init = 0.8 - 0.6 * math.exp(-0.3 * l)
        mod3 = mod[l].reshape(8, N_MOD, D_MODEL)
        z, xbc, lru, dt, gq, gk, gv, dq, dk, dv = _in_proj(
            h, mod3, norm1_g[l].reshape(1, -1), w_in_p, l, gtab, dtab, seg64, seg32,
            _lane_tile(gqa_q_norm_g[l], GROUP_W), _lane_tile(gqa_k_norm_g[l], GROUP_W),
            _lane_tile(diff_q_norm_g[l], GROUP_W), _lane_tile(diff_k_norm_g[l], GROUP_W), nb, t)
        r3 = lambda a: a.reshape(nb, t, a.shape[-1])
        xbc, dt, z, lru = r3(xbc), r3(dt), r3(z), r3(lru)
        cw, cb = ssd_conv_w[l], ssd_conv_b[l].reshape(1, -1)
        lcw, lcb = lru_conv_w[l], lru_conv_b[l].reshape(1, -1)
        ya_f, xact, hf, xc = _scans(
            False, (xbc, dt, z, None, cw, cb, dtb_rows[l], a_rows[l], d_rows[l, 0:1], None),
            (lru, None, None, lcw, lcb, lru_w[l, 0], lru_b[l, 0:1], lru_lambda[l, 0:1]), t)
        ya, yc = _scans(
            True, (xact, dt, z, ya_f, None, None, dtb_rows[l], a_rows[l], d_rows[l, 1:2], ssd_norm_g[l].reshape(1, -1)),
            (lru, xc, hf, None, None, lru_w[l, 1], lru_b[l, 1:2], lru_lambda[l, 1:2]), t)
        last = l == depth - 1
        yb = _attention(False, gq, gk, gv, None, None, 1.0)
        lam_row = jnp.full((1, 128), lam_init, F32) + lam_diff[l]
        yd = _attention(True, dq, dk, dv, lam_row, diff_subln_g[l].reshape(1, -1), 1.0 - lam_init)
        h = _out_mlp(h, ya.reshape(n, -1), yb.reshape(n, -1), yc.reshape(n, -1), yd.reshape(n, -1), mod3,
                     norm2_g[l].reshape(1, -1), w_out_b, w1_b, w2_b, l, tpb, last)
    return h.reshape(nb, s_len, D_MODEL)
```

```python
import functools
import math

import jax
import jax.numpy as jnp
import numpy as np
from jax import lax
from jax.experimental import pallas as pl
from jax.experimental.pallas import tpu as pltpu

F32 = jnp.float32
BF16 = jnp.bfloat16

D_MODEL = 1024
CTX = 256
GROUP_W = 256
D_FF = 4 * D_MODEL
N_MOD = 6
NORM_EPS = 1e-6
ROPE_THETA = 10000.0
GRID_W = 64
CONV_W = 4
CONV_PAD_L = CONV_W // 2
LANES = 128
SUBLANES = 8
HALO = SUBLANES

SSD_HEADS = 4
SSD_P = 64
SSD_GROUPS = 2
SSD_N = 128
SSD_CHUNK = 128
SSD_BLOCK = 256
SSD_XBC = GROUP_W + 2 * SSD_GROUPS * SSD_N
GQA_HEADS = 4
GQA_KV = 2
GQA_D = 64
GQA_COLS = GROUP_W + 2 * GQA_KV * GQA_D
LRU_C = 8.0
LRU_COLS = 2 * GROUP_W
DIFF_HEADS = 4
DIFF_V = 64
DIFF_QK = 32
DIFF_COLS = 3 * GROUP_W
DT_PAD = LANES
ATT_COLS = GQA_COLS + DIFF_COLS
IN_COLS_P = ATT_COLS + GROUP_W + SSD_XBC + LRU_COLS + DT_PAD

MOD_ROWS = SUBLANES
MOD_COL_TILE = 1536
ROW_TILE = 256
LRU_BLOCK = 256
ATT_TQ = 256
KEY_TILE = 256
ATT_MAX_UNROLL = 33
V_PAD = LANES
LOG2E = math.log2(math.e)
VMEM_LIMIT = 56 * 1024 * 1024


def _cparams(sem):
    return pltpu.CompilerParams(dimension_semantics=sem, vmem_limit_bytes=VMEM_LIMIT)


def _sigmoid(x):
    return 1.0 / (1.0 + jnp.exp(-x))


def _silu(x):
    return x * _sigmoid(x)


def _softplus(x):
    return jnp.maximum(x, 0.0) + jnp.log(1.0 + jnp.exp(-jnp.abs(x)))


def _rms(x, g):
    ms = jnp.mean(x * x, axis=-1, keepdims=True)
    return x * lax.rsqrt(ms + NORM_EPS) * g


def _dot(a, b):
    return jnp.dot(a, b, preferred_element_type=F32)


def _dot_nt(a, b):
    return lax.dot_general(a, b, (((1,), (1,)), ((), ())), preferred_element_type=F32)


def _bf16_terms(x, n):
    terms = []
    for _ in range(n):
        t = x.astype(BF16)
        terms.append(t)
        x = x - t.astype(F32)
    return terms


def _dot_exact_lhs(a, x, n=3):
    return sum(_dot(a, t) for t in _bf16_terms(x, n))


def _dot_exact_rhs(x, b, n=2):
    return sum(_dot(t, b) for t in _bf16_terms(x, n))


def _mod_kernel(c_ref, w_ref, b_ref, o_ref):
    act = _silu(c_ref[...]).astype(BF16)
    o_ref[0] = _dot(act, w_ref[0].astype(BF16)) + b_ref[0]


def _modulation(c_rows, w_mod, b_mod):
    depth = w_mod.shape[0]
    tn = MOD_COL_TILE
    return pl.pallas_call(
        _mod_kernel,
        grid=(depth, (N_MOD * D_MODEL) // tn),
        in_specs=[pl.BlockSpec((MOD_ROWS, D_MODEL), lambda l, j: (0, 0)),
                  pl.BlockSpec((1, D_MODEL, tn), lambda l, j: (l, 0, j)),
                  pl.BlockSpec((1, 1, tn), lambda l, j: (l, 0, j))],
        out_specs=pl.BlockSpec((1, MOD_ROWS, tn), lambda l, j: (l, 0, j)),
        out_shape=jax.ShapeDtypeStruct((depth, MOD_ROWS, N_MOD * D_MODEL), F32),
        compiler_params=_cparams(("arbitrary", "arbitrary")),
        name="modulation",
    )(c_rows, w_mod, b_mod.reshape(depth, 1, -1))


def _mod_row(i, tpb, nb):
    return jnp.where(i % tpb == tpb - 1, nb, i // tpb)


def _inproj_kernel(h_ref, mod_ref, g_ref, w_ref, gtab_ref, dtab_ref, seg64_ref, seg32_ref, gq_g_ref, gk_g_ref,
                   dq_g_ref, dk_g_ref, z_ref, xbc_ref, lru_ref, dt_ref, gq_ref, gk_ref, gv_ref, dq_ref, dk_ref,
                   dv_ref):
    x = h_ref[...]
    u = (_rms(x, g_ref[...]) * (1.0 + mod_ref[0, 1:2, :]) + mod_ref[0, 0:1, :]).astype(BF16)
    att = _dot(u, w_ref[0, :, :ATT_COLS])
    _qkv_prep(att[:, :GQA_COLS], att[:, GQA_COLS:], gtab_ref, dtab_ref, seg64_ref, seg32_ref, gq_g_ref, gk_g_ref,
              dq_g_ref, dk_g_ref, gq_ref, gk_ref, gv_ref, dq_ref, dk_ref, dv_ref)
    p = _dot(u, w_ref[0, :, ATT_COLS:])
    c0 = 0
    for ref in (z_ref, xbc_ref, lru_ref, dt_ref):
        w = ref.shape[-1]
        ref[...] = p[:, c0:c0 + w]
        c0 += w


def _in_proj(h, mod3, g, w_in_p, layer, gtab, dtab, seg64, seg32, gq_g, gk_g, dq_g, dk_g, nb, t):
    n = h.shape[0]
    tpb = t // ROW_TILE
    widths = (GROUP_W, SSD_XBC, LRU_COLS, DT_PAD)

    def heads(nh, d):
        return (pl.BlockSpec((1, nh, ROW_TILE, d), lambda i: (i // tpb, 0, i % tpb, 0)),
                jax.ShapeDtypeStruct((nb, nh, t, d), BF16))

    def heads_t(nh, d):
        return (pl.BlockSpec((1, nh, 1, d, ROW_TILE), lambda i: (i // tpb, 0, i % tpb, 0, 0)),
                jax.ShapeDtypeStruct((nb, nh, tpb, d, ROW_TILE), BF16))

    outs = [(pl.BlockSpec((ROW_TILE, w), lambda i: (i, 0)), jax.ShapeDtypeStruct((n, w), F32)) for w in widths]
    outs += [heads(GQA_HEADS, GQA_D), heads_t(GQA_KV, GQA_D), heads(GQA_KV, V_PAD),
             heads(2 * DIFF_HEADS, DIFF_QK), heads_t(2 * DIFF_HEADS, DIFF_QK), heads(DIFF_HEADS, V_PAD)]

    def const(shape):
        return pl.BlockSpec(shape, lambda i: (0,) * len(shape))

    return pl.pallas_call(
        _inproj_kernel,
        grid=(n // ROW_TILE,),
        in_specs=[pl.BlockSpec((ROW_TILE, D_MODEL), lambda i: (i, 0)),
                  pl.BlockSpec((1, N_MOD, D_MODEL), lambda i: (_mod_row(i, tpb, nb), 0, 0)),
                  pl.BlockSpec((1, D_MODEL), lambda i: (0, 0)),
                  pl.BlockSpec((1, D_MODEL, IN_COLS_P), lambda i: (layer, 0, 0)),
                  pl.BlockSpec((3, ROW_TILE, LANES), lambda i: (0, i % tpb, 0)),
                  pl.BlockSpec((3, ROW_TILE, LANES), lambda i: (0, i % tpb, 0)),
                  const((GROUP_W, GROUP_W)), const((GROUP_W, GROUP_W)),
                  const((1, GROUP_W)), const((1, GROUP_W)), const((1, GROUP_W)), const((1, GROUP_W))],
        out_specs=[o[0] for o in outs],
        out_shape=[o[1] for o in outs],
        compiler_params=_cparams(("arbitrary",)),
        name="in_proj",
    )(h, mod3, g, w_in_p, gtab, dtab, seg64, seg32, gq_g, gk_g, dq_g, dk_g)


def _conv_block(ext_ref, x, xp, xn, w_ref, b_ref, rows):
    ext_ref[0:HALO, :] = xp
    ext_ref[HALO:HALO + rows, :] = x
    ext_ref[HALO + rows:2 * HALO + rows, :] = xn
    acc = b_ref[...] + w_ref[0:1, :] * ext_ref[pl.ds(HALO - CONV_PAD_L, rows), :]
    for j in range(1, CONV_W):
        acc = acc + w_ref[j:j + 1, :] * ext_ref[pl.ds(HALO - CONV_PAD_L + j, rows), :]
    return acc


def _halo_specs(nb, rows, width, col_blk, blk_of_step, t):
    per = rows // HALO
    last = t // HALO - 1
    prev = pl.BlockSpec((nb, HALO, width), lambda s: (0, jnp.maximum(blk_of_step(s) * per - 1, 0), col_blk))
    nxt = pl.BlockSpec((nb, HALO, width), lambda s: (0, jnp.minimum((blk_of_step(s) + 1) * per, last), col_blk))
    return prev, nxt


def _ssd_step(rev, n_lat_blocks, *refs):
    if rev:
        xact_ref, dt_ref, z_ref, yf_ref, dtb_ref, a_ref, d_ref, ng_ref, o_ref, h_scr = refs
    else:
        (xbc_ref, xp_ref, xn_ref, dt_ref, cw_ref, cb_ref, dtb_ref, a_ref, d_ref,
         o_ref, xact_ref, h_scr, ext_scr) = refs
    nb = dt_ref.shape[0]
    q = SSD_CHUNK
    n_sub = SSD_BLOCK // q
    sub_order = range(n_sub - 1, -1, -1) if rev else range(n_sub)
    s = pl.program_id(0)
    n_steps = pl.num_programs(0)
    if rev:
        blk = n_steps - 1 - s
    else:
        blk = jnp.where(s == 0, n_lat_blocks, s - 1)
    seg_start = (blk == 0) | (blk == n_lat_blocks)
    seg_end = (blk == n_lat_blocks - 1) | (blk == n_steps - 1)

    li = lax.broadcasted_iota(jnp.int32, (q, q), 0)
    si = lax.broadcasted_iota(jnp.int32, (q, q), 1)
    keep = (li <= si) if rev else (li >= si)
    tri = keep.astype(BF16)
    last = 0 if rev else q - 1
    col0 = SSD_HEADS if rev else 0

    pairs = [(b, k) for k in sub_order for b in range(nb)]
    pre = {}
    for b in range(nb):
        if rev:
            xbc_blk = xact_ref[b]
        else:
            xp = jnp.where(seg_start, 0.0, xp_ref[b])
            xn = jnp.where(seg_end, 0.0, xn_ref[b])
            xbc_blk = _silu(_conv_block(ext_scr, xbc_ref[b], xp, xn, cw_ref, cb_ref, SSD_BLOCK))
            xact_ref[b] = xbc_blk
        for k in range(n_sub):
            sp = _softplus(dt_ref[b, k * q:(k + 1) * q, :] + dtb_ref[...])
            acum = _dot_exact_lhs(tri, sp * a_ref[...])
            pre[b, k] = (xbc_blk[k * q:(k + 1) * q], sp, acum, acum.T)
    ys = {pair: [] for pair in pairs}
    for g in range(SSD_GROUPS):
        grp = {}
        for pair in pairs:
            xbc = pre[pair][0]
            bm = xbc[:, GROUP_W + g * SSD_N:GROUP_W + (g + 1) * SSD_N]
            cm = xbc[:, GROUP_W + (SSD_GROUPS + g) * SSD_N:GROUP_W + (SSD_GROUPS + g + 1) * SSD_N]
            grp[pair] = (bm.T.astype(BF16), cm, _dot_nt(cm.astype(BF16), bm.astype(BF16)))
        for hh in range(g * (SSD_HEADS // SSD_GROUPS), (g + 1) * (SSD_HEADS // SSD_GROUPS)):
            c = col0 + hh
            for pair in pairs:
                b = pair[0]
                xbc, sp, acum, acum_t = pre[pair]
                bm_t, cm, cb = grp[pair]
                ac = acum[:, c:c + 1]
                ar = acum_t[c:c + 1, :]
                tot = acum[last:last + 1, c:c + 1]
                decay = jnp.exp(jnp.where(keep, ac - ar, -jnp.inf))
                xdt = xbc[:, hh * SSD_P:(hh + 1) * SSD_P] * sp[:, c:c + 1]
                y = _dot((cb * decay).astype(BF16), xdt.astype(BF16))
                hin = h_scr[b, hh]
                y = y + _dot((cm * jnp.exp(ac)).astype(BF16), hin.astype(BF16))
                h_scr[b, hh] = hin * jnp.exp(tot) + _dot(bm_t, (xdt * jnp.exp(tot - ac)).astype(BF16))
                ys[pair].append(y)
    for b, k in pairs:
        rows = slice(k * q, (k + 1) * q)
        y = jnp.concatenate(ys[b, k], axis=1) + d_ref[...] * pre[b, k][0][:, 0:GROUP_W]
        if rev:
            y = (y + yf_ref[b, rows, :]) * _silu(z_ref[b, rows, :])
            half = GROUP_W // SSD_GROUPS
            y = jnp.concatenate([_rms(y[:, :half], ng_ref[:, :half]), _rms(y[:, half:], ng_ref[:, half:])], axis=1)
        o_ref[b, rows, :] = y


def _ssd_plan(rev, blk_of, xbc, dt, z, yf, cw, cb, dtb, a_row, d_row, ng, t):
    nb = xbc.shape[0]

    def tok(width):
        return pl.BlockSpec((nb, SSD_BLOCK, width), lambda s: (0, blk_of(s), 0))

    def const(shape):
        return pl.BlockSpec(shape, lambda s: (0,) * len(shape))

    state = pltpu.VMEM((nb, SSD_HEADS, SSD_N, SSD_P), F32)
    small = [const((1, DT_PAD)), const((1, DT_PAD)), const((1, GROUP_W))]
    if rev:
        in_specs = [tok(SSD_XBC), tok(DT_PAD), tok(GROUP_W), tok(GROUP_W)] + small + [const((1, GROUP_W))]
        args = [xbc, dt, z, yf, dtb, a_row, d_row, ng]
        out_specs, out_shape, scratch = [tok(GROUP_W)], [jax.ShapeDtypeStruct((nb, t, GROUP_W), F32)], [state]
    else:
        prev, nxt = _halo_specs(nb, SSD_BLOCK, SSD_XBC, 0, blk_of, t)
        in_specs = [tok(SSD_XBC), prev, nxt, tok(DT_PAD), const((CONV_W, SSD_XBC)), const((1, SSD_XBC))] + small
        args = [xbc, xbc, xbc, dt, cw, cb, dtb, a_row, d_row]
        out_specs = [tok(GROUP_W), tok(SSD_XBC)]
        out_shape = [jax.ShapeDtypeStruct((nb, t, GROUP_W), F32), jax.ShapeDtypeStruct((nb, t, SSD_XBC), F32)]
        scratch = [state, pltpu.VMEM((SSD_BLOCK + 2 * HALO, SSD_XBC), F32)]
    return in_specs, args, out_specs, out_shape, scratch


def _lru_step(rev, n_lat_blocks, *refs):
    if rev:
        xc_ref, gate_ref, hf_ref, w_ref, bias_ref, lam_ref, o_ref, h_scr, a_scr, u_scr = refs
    else:
        (x_ref, xp_ref, xn_ref, cw_ref, cb_ref, w_ref, bias_ref, lam_ref,
         o_ref, xc_ref, h_scr, a_scr, u_scr, ext_scr) = refs
    nb = o_ref.shape[0]
    rows = LRU_BLOCK
    s = pl.program_id(0)
    n_steps = pl.num_programs(0)
    if rev:
        blk = n_steps - 1 - s
    else:
        blk = jnp.where(s == 0, n_lat_blocks, s - 1)
    seg_start = (blk == 0) | (blk == n_lat_blocks)
    seg_end = (blk == n_lat_blocks - 1) | (blk == n_lat_blocks)

    sp_lam = _softplus(-lam_ref[...])
    for b in range(nb):
        if rev:
            xc = xc_ref[b]
        else:
            xp = jnp.where(seg_start, 0.0, xp_ref[b])
            xn = jnp.where(seg_end, 0.0, xn_ref[b])
            xc = _conv_block(ext_scr, x_ref[b], xp, xn, cw_ref, cb_ref, rows)
            xc_ref[b] = xc
        ri = _dot(xc.astype(BF16), w_ref[...]) + bias_ref[...]
        r = _sigmoid(ri[:, :GROUP_W])
        gi = _sigmoid(ri[:, GROUP_W:])
        log_a = -LRU_C * r * sp_lam
        a_scr[b] = jnp.exp(log_a)
        u_scr[b] = jnp.sqrt(1.0 - jnp.exp(2.0 * log_a)) * (gi * xc)

    tile = SUBLANES
    row_id = lax.broadcasted_iota(jnp.int32, (tile, GROUP_W), 0)

    def tile_scan(a, u):
        d = 1
        while d < tile:
            shift, valid = (tile - d, row_id < tile - d) if rev else (d, row_id >= d)
            u = u + a * jnp.where(valid, pltpu.roll(u, shift, 0), 0.0)
            a = a * jnp.where(valid, pltpu.roll(a, shift, 0), 1.0)
            d *= 2
        return a, u

    hs = [h_scr[b] for b in range(nb)]
    for g in range(rows // tile):
        r0 = ((rows // tile - 1 - g) if rev else g) * tile
        for b in range(nb):
            a, u = tile_scan(a_scr[b, r0:r0 + tile, :], u_scr[b, r0:r0 + tile, :])
            h = u + a * hs[b]
            u_scr[b, r0:r0 + tile, :] = h
            hs[b] = h[0:1] if rev else h[tile - 1:tile]
    for b in range(nb):
        h_scr[b] = hs[b]
        if rev:
            gt = gate_ref[b]
            gelu = 0.5 * gt * (1.0 + jnp.tanh(math.sqrt(2.0 / math.pi) * (gt + 0.044715 * gt * gt * gt)))
            o_ref[b] = gelu * (hf_ref[b] + u_scr[b])
        else:
            o_ref[b] = u_scr[b]


def _lru_plan(rev, blk_of, lru, xc, hf, cw, cb, w_dir, bias_dir, lam_dir, t):
    nb = lru.shape[0]

    def tok(col_blk):
        return pl.BlockSpec((nb, LRU_BLOCK, GROUP_W), lambda s: (0, blk_of(s), col_blk))

    def const(shape):
        return pl.BlockSpec(shape, lambda s: (0,) * len(shape))

    gate_w = [const((GROUP_W, 2 * GROUP_W)), const((1, 2 * GROUP_W)), const((1, GROUP_W))]
    scratch = [pltpu.VMEM((nb, 1, GROUP_W), F32), pltpu.VMEM((nb, LRU_BLOCK, GROUP_W), F32),
               pltpu.VMEM((nb, LRU_BLOCK, GROUP_W), F32)]
    out_tok = pl.BlockSpec((nb, LRU_BLOCK, GROUP_W), lambda s: (0, blk_of(s), 0))
    out_sds = jax.ShapeDtypeStruct((nb, t, GROUP_W), F32)
    if rev:
        in_specs = [out_tok, tok(0), out_tok] + gate_w
        args = [xc, lru, hf, w_dir, bias_dir, lam_dir]
        out_specs, out_shape = [out_tok], [out_sds]
    else:
        prev, nxt = _halo_specs(nb, LRU_BLOCK, GROUP_W, 1, blk_of, t)
        in_specs = [tok(1), prev, nxt, const((CONV_W, GROUP_W)), const((1, GROUP_W))] + gate_w
        args = [lru, lru, lru, cw, cb, w_dir, bias_dir, lam_dir]
        out_specs, out_shape = [out_tok, out_tok], [out_sds, out_sds]
        scratch.append(pltpu.VMEM((LRU_BLOCK + 2 * HALO, GROUP_W), F32))
    return in_specs, args, out_specs, out_shape, scratch


def _scan_kernel(rev, n_lat_blocks, counts, *refs):
    (n_in_s, n_in_l), (n_out_s, n_out_l), (n_scr_s, _) = counts
    n_in, n_out = n_in_s + n_in_l, n_out_s + n_out_l
    ins, outs, scr = refs[:n_in], refs[n_in:n_in + n_out], refs[n_in + n_out:]

    @pl.when(pl.program_id(0) == 0)
    def _():
        for state in (scr[0], scr[n_scr_s]):
            state[...] = jnp.zeros_like(state)

    _ssd_step(rev, n_lat_blocks, *ins[:n_in_s], *outs[:n_out_s], *scr[:n_scr_s])
    _lru_step(rev, n_lat_blocks, *ins[n_in_s:], *outs[n_out_s:], *scr[n_scr_s:])


def _scans(rev, ssd_operands, lru_operands, t):
    assert CTX == SSD_BLOCK == LRU_BLOCK
    n_blocks = t // SSD_BLOCK
    n_lat_blocks = (t - CTX) // SSD_BLOCK

    def blk_of(s):
        if rev:
            return n_blocks - 1 - s
        return jnp.where(s == 0, n_lat_blocks, s - 1)

    ssd = _ssd_plan(rev, blk_of, *ssd_operands, t)
    lru = _lru_plan(rev, blk_of, *lru_operands, t)
    counts = tuple((len(a), len(b)) for a, b in zip(ssd, lru))
    counts = (counts[0], counts[2], counts[4])
    return pl.pallas_call(
        functools.partial(_scan_kernel, rev, n_lat_blocks, counts),
        grid=(n_blocks,),
        in_specs=ssd[0] + lru[0],
        out_specs=ssd[2] + lru[2],
        out_shape=ssd[3] + lru[3],
        scratch_shapes=ssd[4] + lru[4],
        compiler_params=_cparams(("arbitrary",)),
        name="scan_bwd" if rev else "scan_fwd",
    )(*ssd[1], *lru[1])


def _seg_mean(x2, seg_ref):
    return _dot_exact_rhs(x2, seg_ref[...])


def _rope(x, cos, sin_lo, sin_hi, half):
    w = x.shape[-1]
    rep = w // cos.shape[-1]
    cos, sin_lo, sin_hi = (jnp.concatenate([t] * rep, axis=1) if rep > 1 else t for t in (cos, sin_lo, sin_hi))
    return x * cos + pltpu.roll(x, w - half, 1) * sin_lo + pltpu.roll(x, half, 1) * sin_hi


def _qkv_prep(gqa, diff, gtab_ref, dtab_ref, seg64_ref, seg32_ref, gq_g_ref, gk_g_ref,
              dq_g_ref, dk_g_ref, gq_ref, gk_ref, gv_ref, dq_ref, dk_ref, dv_ref):
    def norm_rope(x, seg_ref, g, tab_ref, half, scale):
        w = x.shape[-1]
        xn = x * lax.rsqrt(_seg_mean(x * x, seg_ref)[:, :w] + NORM_EPS) * g[:, :w]
        xr = _rope(xn, tab_ref[0], tab_ref[1], tab_ref[2], half)
        return xr * scale if scale != 1.0 else xr

    def scatter(ref, x, n, d):
        for hh in range(n):
            ref[0, hh] = x[:, hh * d:(hh + 1) * d].astype(ref.dtype)

    def scatter_v(ref, x, n, d):
        ones = jnp.ones((x.shape[0], V_PAD - d), F32)
        for hh in range(n):
            ref[0, hh] = jnp.concatenate([x[:, hh * d:(hh + 1) * d], ones], axis=1).astype(ref.dtype)

    def scatter_t(ref, x, n, d):
        xt = x.T
        for hh in range(n):
            ref[0, hh, 0] = xt[hh * d:(hh + 1) * d, :].astype(ref.dtype)

    kw = GQA_KV * GQA_D
    q = norm_rope(gqa[:, :GROUP_W], seg64_ref, gq_g_ref[...], gtab_ref, GQA_D // 4, GQA_D ** -0.5 * LOG2E)
    k = _pad_lanes_rope(gqa[:, GROUP_W:GROUP_W + kw], seg64_ref, gk_g_ref[...], gtab_ref, GQA_D // 4)
    scatter(gq_ref, q, GQA_HEADS, GQA_D)
    scatter_t(gk_ref, k, GQA_KV, GQA_D)
    scatter_v(gv_ref, gqa[:, GROUP_W + kw:], GQA_KV, GQA_D)
    dq = norm_rope(diff[:, :GROUP_W], seg32_ref, dq_g_ref[...], dtab_ref, DIFF_QK // 4, DIFF_QK ** -0.5 * LOG2E)
    dk = norm_rope(diff[:, GROUP_W:2 * GROUP_W], seg32_ref, dk_g_ref[...], dtab_ref, DIFF_QK // 4, 1.0)
    scatter(dq_ref, dq, 2 * DIFF_HEADS, DIFF_QK)
    scatter_t(dk_ref, dk, 2 * DIFF_HEADS, DIFF_QK)
    scatter_v(dv_ref, diff[:, 2 * GROUP_W:], DIFF_HEADS, DIFF_V)


def _pad_lanes_rope(x, seg_ref, g, tab_ref, half):
    w = x.shape[-1]
    xn = x * lax.rsqrt(_dot_exact_rhs(x * x, seg_ref[:w, :w]) + NORM_EPS) * g[:, :w]
    return _rope(xn, tab_ref[0], tab_ref[1], tab_ref[2], half)


def _attn_kernel(diff_mode, post_scale, unroll, q_ref, kt_ref, v_ref, *rest):
    if diff_mode:
        lam_ref, sg_ref, o_ref, s_scr, m_scr, acc_scr = rest
    else:
        o_ref, s_scr, m_scr, acc_scr = rest
    tq = q_ref.shape[2]
    n_tiles = kt_ref.shape[2]
    n_units = v_ref.shape[1]
    dv = o_ref.shape[-1] // 2

    def over_tiles(ctx_only, fn):
        if ctx_only:
            fn([n_tiles - 1])
        else:
            def body(i, carry):
                fn([i * unroll + j for j in range(unroll)])
                return carry
            lax.fori_loop(0, n_tiles // unroll, body, 0)

    def run_unit(u, ctx_only):
        if diff_mode:
            blocks = [(j * tq, q_ref[0, 2 * u + j], 2 * u + j) for j in range(2)]
        else:
            blocks = [(0, q_ref[0].reshape(2 * tq, q_ref.shape[-1]), u)]
        m_scr[...] = jnp.full_like(m_scr, -jnp.inf)

        def scores(kts):
            for kt in kts:
                for r0, q, kh in blocks:
                    r = q.shape[0]
                    s = _dot(q, kt_ref[0, kh, kt])
                    s_scr[kt, r0:r0 + r, :] = s
                    m_scr[r0:r0 + r, :] = jnp.maximum(m_scr[r0:r0 + r, :], jnp.maximum(s[:, :LANES], s[:, LANES:]))

        over_tiles(ctx_only, scores)
        m_scr[...] = jnp.broadcast_to(jnp.max(m_scr[...], axis=-1, keepdims=True), m_scr.shape)
        acc_scr[...] = jnp.zeros_like(acc_scr)

        def accumulate(kts):
            acc = None
            for kt in kts:
                m_rep = m_scr[...]
                p = jnp.exp2(s_scr[kt] - jnp.concatenate([m_rep, m_rep], axis=1))
                off = kt * KEY_TILE if isinstance(kt, int) else pl.multiple_of(kt * KEY_TILE, KEY_TILE)
                d = _dot(p.astype(BF16), v_ref[0, u, pl.ds(off, KEY_TILE), :])
                acc = d if acc is None else acc + d
            acc_scr[...] += acc

        over_tiles(ctx_only, accumulate)
        acc = acc_scr[...]
        return acc[:, :dv] / acc[:, dv:dv + 1]

    def finish(ctx_only):
        outs = []
        for u in range(n_units):
            o = run_unit(u, ctx_only)
            if diff_mode:
                o = o[:tq] - lam_ref[:, :o.shape[-1]] * o[tq:]
                outs.append(_rms(o, sg_ref[...]) * post_scale)
            else:
                outs += [o[:tq], o[tq:]]
        o_ref[0] = jnp.concatenate(outs, axis=1)

    is_ctx = pl.program_id(2) == pl.num_programs(2) - 1

    @pl.when(is_ctx)
    def _():
        finish(True)

    @pl.when(jnp.logical_not(is_ctx))
    def _():
        finish(False)


def _attention(diff_mode, q, kt, v, lam_row, subln_g, post_scale):
    nb, _, t, dqk = q.shape
    dv = GROUP_W // (GQA_HEADS if not diff_mode else DIFF_HEADS)
    n_tiles = kt.shape[2]
    n_units = 2 if diff_mode else 1
    n_groups = v.shape[1] // n_units
    qh = q.shape[1] // n_groups
    kh = kt.shape[1] // n_groups
    assert kt.shape[-1] == KEY_TILE
    unroll = max(u for u in range(1, ATT_MAX_UNROLL + 1) if n_tiles % u == 0)
    in_specs = [pl.BlockSpec((1, qh, ATT_TQ, dqk), lambda b, g, i: (b, g, i, 0)),
                pl.BlockSpec((1, kh, n_tiles, dqk, KEY_TILE), lambda b, g, i: (b, g, 0, 0, 0)),
                pl.BlockSpec((1, n_units, t, V_PAD), lambda b, g, i: (b, g, 0, 0))]
    args = [q, kt, v]
    if diff_mode:
        in_specs += [pl.BlockSpec((1, LANES), lambda b, g, i: (0, 0)), pl.BlockSpec((1, dv), lambda b, g, i: (0, 0))]
        args += [lam_row, subln_g]
    return pl.pallas_call(
        functools.partial(_attn_kernel, diff_mode, post_scale, unroll),
        grid=(nb, n_groups, t // ATT_TQ),
        in_specs=in_specs,
        out_specs=pl.BlockSpec((1, ATT_TQ, 2 * dv), lambda b, g, i: (b, i, g)),
        out_shape=jax.ShapeDtypeStruct((nb, t, GROUP_W), F32),
        scratch_shapes=[pltpu.VMEM((n_tiles, 2 * ATT_TQ, KEY_TILE), F32), pltpu.VMEM((2 * ATT_TQ, LANES), F32),
                        pltpu.VMEM((2 * ATT_TQ, V_PAD), F32)],
        compiler_params=_cparams(("arbitrary", "arbitrary", "arbitrary")),
        name="diff_attn" if diff_mode else "gqa_attn",
    )(*args)


def _outmlp_kernel(h_ref, ya_ref, yb_ref, yc_ref, yd_ref, mod_ref, g_ref, wo_ref, w1_ref, w2_ref, o_ref):
    mix = jnp.concatenate([ya_ref[...], yb_ref[...], yc_ref[...], yd_ref[...]], axis=1).astype(BF16)
    h1 = h_ref[...] + mod_ref[0, 2:3, :] * _dot(mix, wo_ref[0])
    v = _rms(h1, g_ref[...]) * (1.0 + mod_ref[0, 4:5, :]) + mod_ref[0, 3:4, :]
    u = jnp.maximum(_dot(v.astype(BF16), w1_ref[0]), 0.0)
    o_ref[...] = h1 + mod_ref[0, 5:6, :] * _dot((u * u).astype(BF16), w2_ref[0])


def _out_mlp(h, ya, yb, yc, yd, mod3, g2, w_out, w1, w2, layer, tpb, latent_only):
    n = h.shape[0]
    lat = tpb - 1
    nb = (n // ROW_TILE) // tpb
    n_tiles = nb * lat if latent_only else n // ROW_TILE

    def src(i):
        return (i // lat) * tpb + i % lat if latent_only else i

    def mod_row(i):
        return i // lat if latent_only else _mod_row(i, tpb, nb)

    def tok(w):
        return pl.BlockSpec((ROW_TILE, w), lambda i: (src(i), 0))

    def weight(k, m):
        return pl.BlockSpec((1, k, m), lambda i: (layer, 0, 0), pipeline_mode=pl.Buffered(1))

    return pl.pallas_call(
        _outmlp_kernel,
        grid=(n_tiles,),
        in_specs=[tok(D_MODEL), tok(GROUP_W), tok(GROUP_W), tok(GROUP_W), tok(GROUP_W),
                  pl.BlockSpec((1, N_MOD, D_MODEL), lambda i: (mod_row(i), 0, 0)),
                  pl.BlockSpec((1, D_MODEL), lambda i: (0, 0)),
                  weight(D_MODEL, D_MODEL), weight(D_MODEL, D_FF), weight(D_FF, D_MODEL)],
        out_specs=pl.BlockSpec((ROW_TILE, D_MODEL), lambda i: (i, 0)),
        out_shape=jax.ShapeDtypeStruct((n_tiles * ROW_TILE, D_MODEL), F32),
        compiler_params=_cparams(("arbitrary",)),
        name="out_mlp",
    )(h, ya, yb, yc, yd, mod3, g2, w_out, w1, w2)


def _rope_tables(s_len, head_dim, lanes=LANES):
    m = head_dim // 2
    half = m // 2
    lane = np.arange(lanes)
    d = lane % head_dim
    freq = ROPE_THETA ** (-(d % half).astype(np.float64) / half)
    t = np.arange(s_len)
    pos = np.where((d < m)[None, :], (t // GRID_W)[:, None], (t % GRID_W)[:, None]).astype(np.float32)
    ang = pos * freq.astype(np.float32)[None, :]
    cos, sin = np.cos(ang), np.sin(ang)
    low = ((d % m) < half)[None, :]
    tab = np.stack([cos, np.where(low, -sin, 0.0), np.where(low, 0.0, sin)])
    ident = np.stack([np.ones((CTX, lanes)), np.zeros((CTX, lanes)), np.zeros((CTX, lanes))])
    return jnp.asarray(np.concatenate([tab, ident], axis=1), F32)


def _seg_matrix(width, seg):
    idx = np.arange(width) // seg
    return jnp.asarray((idx[:, None] == idx[None, :]).astype(np.float32) / seg, BF16)


def _lane_tile(v, width):
    return jnp.tile(v, width // v.shape[-1]).reshape(1, width)


def _block_diag(w):
    nblk, j, k = w.shape
    eye = jnp.eye(nblk, dtype=w.dtype)
    return jnp.einsum('njk,nm->njmk', w, eye).reshape(nblk * j, nblk * k)


def kernel(x, c, ctx, c_ctx, w_mod, b_mod, norm1_g, w_in, ssd_conv_w, ssd_conv_b, ssd_a_log, ssd_dt_bias, ssd_d, ssd_norm_g, gqa_q_norm_g, gqa_k_norm_g, lru_conv_w, lru_conv_b, lru_w_r, lru_b_r, lru_w_i, lru_b_i, lru_lambda, diff_q_norm_g, diff_k_norm_g, diff_lambda_q1, diff_lambda_k1, diff_lambda_q2, diff_lambda_k2, diff_subln_g, w_out, norm2_g, w_mlp1, w_mlp2):
    nb, s_len, d_model = x.shape
    depth = w_mod.shape[0]
    assert d_model == D_MODEL and ctx.shape[1] == CTX and s_len % ROW_TILE == 0 and s_len % GRID_W == 0
    t = s_len + CTX
    tpb = t // ROW_TILE
    n = nb * t

    zx, dt_end = GROUP_W + SSD_XBC, GROUP_W + SSD_XBC + 2 * SSD_HEADS
    gqa_end, lru_end = dt_end + GQA_COLS, dt_end + GQA_COLS + LRU_COLS
    w_in_p = jnp.concatenate(
        [w_in[:, :, dt_end:gqa_end], w_in[:, :, lru_end:], w_in[:, :, :zx], w_in[:, :, gqa_end:lru_end],
         w_in[:, :, zx:dt_end], jnp.zeros((depth, D_MODEL, DT_PAD - 2 * SSD_HEADS), w_in.dtype)],
        axis=-1).astype(BF16)
    w_out_b, w1_b, w2_b = w_out.astype(BF16), w_mlp1.astype(BF16), w_mlp2.astype(BF16)
    lru_w = jnp.stack([jnp.concatenate([_block_diag(lru_w_r[l, d]), _block_diag(lru_w_i[l, d])], axis=1)
                       for l in range(depth) for d in range(2)]).reshape(depth, 2, GROUP_W, 2 * GROUP_W).astype(BF16)
    lru_b = jnp.concatenate([lru_b_r, lru_b_i], axis=-1)
    pad8 = DT_PAD - 2 * SSD_HEADS
    dtb_rows = jnp.pad(ssd_dt_bias.reshape(depth, 1, 2 * SSD_HEADS), ((0, 0), (0, 0), (0, pad8)))
    a_rows = jnp.pad(-jnp.exp(ssd_a_log.reshape(depth, 1, 2 * SSD_HEADS)), ((0, 0), (0, 0), (0, pad8)))
    d_rows = jnp.repeat(ssd_d, SSD_P, axis=-1)
    lam_diff = (jnp.exp(jnp.sum(diff_lambda_q1 * diff_lambda_k1, axis=-1))
                - jnp.exp(jnp.sum(diff_lambda_q2 * diff_lambda_k2, axis=-1)))
    gtab = _rope_tables(s_len, GQA_D)
    dtab = _rope_tables(s_len, DIFF_QK)
    seg64 = _seg_matrix(GROUP_W, GQA_D)
    seg32 = _seg_matrix(GROUP_W, DIFF_QK)

    assert nb + 1 <= MOD_ROWS
    c_rows = jnp.concatenate([c, c_ctx[None, :], jnp.zeros((MOD_ROWS - nb - 1, D_MODEL), F32)], axis=0)
    mod = _modulation(c_rows, w_mod, b_mod)
    h = jnp.concatenate([x, ctx], axis=1).reshape(n, D_MODEL)

    for l in range(depth):
        lam_init = 0.8 - 0.6 * math.exp(-0.3 * l)
        mod3 = mod[l].reshape(MOD_ROWS, N_MOD, D_MODEL)
        z, xbc, lru, dt, gq, gk, gv, dq, dk, dv = _in_proj(
            h, mod3, norm1_g[l].reshape(1, -1), w_in_p, l, gtab, dtab, seg64, seg32,
            _lane_tile(gqa_q_norm_g[l], GROUP_W), _lane_tile(gqa_k_norm_g[l], GROUP_W),
            _lane_tile(diff_q_norm_g[l], GROUP_W), _lane_tile(diff_k_norm_g[l], GROUP_W), nb, t)
        r3 = lambda a: a.reshape(nb, t, a.shape[-1])
        xbc, dt, z, lru = r3(xbc), r3(dt), r3(z), r3(lru)
        cw, cb = ssd_conv_w[l], ssd_conv_b[l].reshape(1, -1)
        lcw, lcb = lru_conv_w[l], lru_conv_b[l].reshape(1, -1)
        ya_f, xact, hf, xc = _scans(
            False, (xbc, dt, z, None, cw, cb, dtb_rows[l], a_rows[l], d_rows[l, 0:1], None),
            (lru, None, None, lcw, lcb, lru_w[l, 0], lru_b[l, 0:1], lru_lambda[l, 0:1]), t)
        ya, yc = _scans(
            True, (xact, dt, z, ya_f, None, None, dtb_rows[l], a_rows[l], d_rows[l, 1:2], ssd_norm_g[l].reshape(1, -1)),
            (lru, xc, hf, None, None, lru_w[l, 1], lru_b[l, 1:2], lru_lambda[l, 1:2]), t)
        last = l == depth - 1
        yb = _attention(False, gq, gk, gv, None, None, 1.0)
        lam_row = jnp.full((1, LANES), lam_init, F32) + lam_diff[l]
        yd = _attention(True, dq, dk, dv, lam_row, diff_subln_g[l].reshape(1, -1), 1.0 - lam_init)
        h = _out_mlp(h, ya.reshape(n, -1), yb.reshape(n, -1), yc.reshape(n, -1), yd.reshape(n, -1), mod3,
                     norm2_g[l].reshape(1, -1), w_out_b, w1_b, w2_b, l, tpb, last)
    return h.reshape(nb, s_len, D_MODEL)
```
